```python
import math
import jax
import jax.numpy as jnp
from jax import lax
import numpy as np

D_MODEL = 1024
BATCH = 32
SEQ = 2048
DEPTH = 2
DEC_BATCH = 16
DEC_SEQ = 64
PAST_LEN = 4096

CHUNK = 64
EPS = 1e-6
F32 = jnp.float32

MIX_WIDTH = D_MODEL
MIX_A = MIX_WIDTH // 4
MIX_B = MIX_WIDTH // 4
MIX_C = MIX_WIDTH // 4
MIX_D = MIX_WIDTH - MIX_A - MIX_B - MIX_C

SSD_HEAD_DIM = 64
SSD_HEADS = MIX_A // SSD_HEAD_DIM
SSD_GROUPS = 2
SSD_STATE = 64
SSD_CONV = 4
SSD_CONV_CH = MIX_A + 2 * SSD_GROUPS * SSD_STATE
SSD_IN = MIX_A + SSD_CONV_CH + SSD_HEADS

CCM_WIDTH = 31
CCM_IN = 2 * MIX_B

ATT_HEAD_DIM = 64
ATT_HEADS = MIX_C // ATT_HEAD_DIM
ATT_PREV_CHUNKS = 8
ATT_WINDOW = ATT_PREV_CHUNKS * CHUNK
ATT_BAND = ATT_WINDOW + CHUNK
ATT_IN = 3 * MIX_C
ATT_SCALE = ATT_HEAD_DIM ** -0.5
REL_MAX = 128

S5_GROUP_CH = 16
S5_GROUPS = MIX_D // S5_GROUP_CH
S5_STATE = 64

IN_WIDTH = SSD_IN + CCM_IN + ATT_IN + MIX_D

MOE_GROUPS = 4
MOE_EXPERTS_PER_GROUP = 8
MOE_EXPERTS = MOE_GROUPS * MOE_EXPERTS_PER_GROUP
MOE_TOP_K = 2
MOE_D_FF = D_MODEL // 2
MOE_BLOCK = 128

kernel_name = 'hybrid_streaming_encoder_step'


def _rmsnorm(x, g):
    xf = x.astype(F32)
    y = xf * lax.rsqrt(jnp.mean(xf * xf, axis=-1, keepdims=True) + EPS)
    return (y * g.astype(F32)).astype(x.dtype)


def _layernorm(x, g, b):
    xf = x.astype(F32)
    mu = jnp.mean(xf, axis=-1, keepdims=True)
    var = jnp.mean(jnp.square(xf - mu), axis=-1, keepdims=True)
    return ((xf - mu) * lax.rsqrt(var + EPS) * g.astype(F32) + b.astype(F32)).astype(x.dtype)


def _causal_dwconv(hist, x, w, b):
    xp = jnp.concatenate([hist.astype(x.dtype), x], axis=1)
    y = lax.conv_general_dilated(xp, w.astype(x.dtype)[:, None, :], (1,), 'VALID',
                                 dimension_numbers=('NWC', 'WIO', 'NWC'),
                                 feature_group_count=x.shape[-1])
    return y + b.astype(x.dtype), xp[:, xp.shape[1] - (w.shape[0] - 1):]


def _segsum(a):
    t = a.shape[-1]
    ar = jnp.broadcast_to(a[..., None], a.shape + (t,))
    ar = jnp.where(jnp.tril(jnp.ones((t, t), bool), -1), ar, 0.0)
    s = jnp.cumsum(ar, axis=-2)
    return jnp.where(jnp.tril(jnp.ones((t, t), bool)), s, -jnp.inf)


def _ssd_scan(x, a, bm, cm, h0):
    b, l, h, p = x.shape
    n = bm.shape[-1]
    cl = min(CHUNK, l)
    nc = l // cl
    x = x.reshape(b, nc, cl, h, p)
    bm = bm.reshape(b, nc, cl, h, n)
    cm = cm.reshape(b, nc, cl, h, n)
    a = a.reshape(b, nc, cl, h).transpose(0, 3, 1, 2)
    a_cs = jnp.cumsum(a, axis=-1)
    decay_in = jnp.exp(_segsum(a))
    cb = jnp.einsum('bclhn,bcshn->bhcls', cm, bm)
    y_diag = jnp.einsum('bhcls,bcshp->bclhp', cb * decay_in, x)
    decay_states = jnp.exp(a_cs[..., -1:] - a_cs)
    states = jnp.einsum('bclhn,bhcl,bclhp->bchpn', bm, decay_states, x)
    states = jnp.concatenate([h0[:, None], states], axis=1)
    decay_chunk = jnp.exp(_segsum(jnp.pad(a_cs[..., -1], ((0, 0), (0, 0), (1, 0)))))
    new_states = jnp.einsum('bhzc,bchpn->bzhpn', decay_chunk, states)
    y_off = jnp.einsum('bclhn,bchpn,bhcl->bclhp', cm, new_states[:, :-1], jnp.exp(a_cs))
    return (y_diag + y_off).reshape(b, l, h, p), new_states[:, -1]


def _ssd_mixer(u, conv_hist, h0, p):
    b, l, _ = u.shape
    z, xbc, dt = jnp.split(u, [MIX_A, MIX_A + SSD_CONV_CH], axis=-1)
    xbc, conv_state = _causal_dwconv(conv_hist, xbc, p['ssd_conv_w'], p['ssd_conv_b'])
    xbc = jax.nn.silu(xbc.astype(F32))
    xs, bm, cm = jnp.split(xbc, [MIX_A, MIX_A + SSD_GROUPS * SSD_STATE], axis=-1)
    xs = xs.reshape(b, l, SSD_HEADS, SSD_HEAD_DIM)
    rep = SSD_HEADS // SSD_GROUPS
    bm = jnp.repeat(bm.reshape(b, l, SSD_GROUPS, SSD_STATE), rep, axis=2)
    cm = jnp.repeat(cm.reshape(b, l, SSD_GROUPS, SSD_STATE), rep, axis=2)
    dt = jax.nn.softplus(dt.astype(F32) + p['ssd_dt_bias'].astype(F32))
    a = -jnp.exp(p['ssd_a_log'].astype(F32))
    y, h_last = _ssd_scan(xs * dt[..., None], dt * a, bm, cm, h0.astype(F32))
    y = y + p['ssd_d'].astype(F32)[:, None] * xs
    y = y.reshape(b, l, MIX_A) * jax.nn.silu(z.astype(F32))
    yg = y.reshape(b, l, SSD_GROUPS, MIX_A // SSD_GROUPS)
    yg = yg * lax.rsqrt(jnp.mean(yg * yg, axis=-1, keepdims=True) + EPS)
    y = yg.reshape(b, l, MIX_A) * p['ssd_norm_w'].astype(F32)
    return y.astype(u.dtype), conv_state, h_last


def _conv_module(u, hist, p):
    a, g = jnp.split(u, 2, axis=-1)
    v = a * jax.nn.sigmoid(g)
    v, new_hist = _causal_dwconv(hist, v, p['ccm_dw_w'], p['ccm_dw_b'])
    v = jax.nn.silu(_layernorm(v, p['ccm_ln_g'], p['ccm_ln_b']))
    return v @ p['ccm_pw_w'] + p['ccm_pw_b'], new_hist


def _rel_bias(table, n_q, n_k, offset):
    d = offset + jnp.arange(n_q)[:, None] - jnp.arange(n_k)[None, :]
    idx = jnp.clip(d, -REL_MAX, REL_MAX) + REL_MAX
    return table.astype(F32)[:, idx]


def _split_qkv(u):
    b, l, _ = u.shape
    q, k, v = jnp.split(u, 3, axis=-1)
    shp = (b, l, ATT_HEADS, ATT_HEAD_DIM)
    return q.reshape(shp), k.reshape(shp), v.reshape(shp)


def _band_attention_prompt(u, table):
    q, k, v = _split_qkv(u)
    b, l = q.shape[:2]
    nc = l // CHUNK
    pad = jnp.zeros((b, ATT_WINDOW, ATT_HEADS, ATT_HEAD_DIM), k.dtype)

    def band(t):
        tp = jnp.concatenate([pad, t], axis=1).reshape(b, nc + ATT_PREV_CHUNKS, CHUNK, ATT_HEADS, ATT_HEAD_DIM)
        return jnp.concatenate([tp[:, j:j + nc] for j in range(ATT_PREV_CHUNKS + 1)], axis=2)

    kb, vb = band(k), band(v)
    qc = q.reshape(b, nc, CHUNK, ATT_HEADS, ATT_HEAD_DIM)
    s = jnp.einsum('bcqhd,bckhd->bchqk', qc, kb, preferred_element_type=F32) * ATT_SCALE
    s = s + _rel_bias(table, CHUNK, ATT_BAND, ATT_WINDOW)
    key_chunk = jnp.arange(nc)[:, None] - ATT_PREV_CHUNKS + jnp.arange(ATT_BAND)[None, :] // CHUNK
    s = jnp.where((key_chunk >= 0)[None, :, None, None, :], s, -jnp.inf)
    pr = jax.nn.softmax(s, axis=-1).astype(v.dtype)
    o = jnp.einsum('bchqk,bckhd->bcqhd', pr, vb).reshape(b, l, MIX_C)
    keep = min(ATT_WINDOW, l)
    return o, k[:, l - keep:], v[:, l - keep:]


def _band_attention_sample(u, k_cache, v_cache, table):
    q, k, v = _split_qkv(u)
    b, l = q.shape[:2]
    kk = jnp.concatenate([k_cache.astype(k.dtype), k], axis=1)
    vv = jnp.concatenate([v_cache.astype(v.dtype), v], axis=1)
    s = jnp.einsum('bqhd,bkhd->bhqk', q, kk, preferred_element_type=F32) * ATT_SCALE
    s = s + _rel_bias(table, l, kk.shape[1], k_cache.shape[1])
    pr = jax.nn.softmax(s, axis=-1).astype(v.dtype)
    o = jnp.einsum('bhqk,bkhd->bqhd', pr, vv).reshape(b, l, MIX_C)
    return o, k, v


def _complex_affine_combine(e1, e2):
    a1r, a1i, b1r, b1i = e1
    a2r, a2i, b2r, b2i = e2
    return (a2r * a1r - a2i * a1i, a2r * a1i + a2i * a1r,
            a2r * b1r - a2i * b1i + b2r, a2r * b1i + a2i * b1r + b2i)


def _s5_mixer(u, h0_re, h0_im, p):
    b, l, _ = u.shape
    uf = u.astype(F32).reshape(b, l, S5_GROUPS, S5_GROUP_CH)
    lr = p['s5_lam_re'].astype(F32)
    li = p['s5_lam_im'].astype(F32)
    dt = jnp.exp(p['s5_log_dt'].astype(F32))[:, None]
    mag = jnp.exp(lr * dt)
    ar, ai = mag * jnp.cos(li * dt), mag * jnp.sin(li * dt)
    den = lr * lr + li * li
    cr = ((ar - 1.0) * lr + ai * li) / den
    ci = (ai * lr - (ar - 1.0) * li) / den
    br, bi = p['s5_b_re'].astype(F32), p['s5_b_im'].astype(F32)
    bbr = cr[..., None] * br - ci[..., None] * bi
    bbi = cr[..., None] * bi + ci[..., None] * br
    xr = jnp.einsum('blgc,gnc->blgn', uf, bbr)
    xi = jnp.einsum('blgc,gnc->blgn', uf, bbi)
    h0r, h0i = h0_re.astype(F32), h0_im.astype(F32)
    xr = xr.at[:, 0].add(ar * h0r - ai * h0i)
    xi = xi.at[:, 0].add(ar * h0i + ai * h0r)
    a_r = jnp.broadcast_to(ar, xr.shape)
    a_i = jnp.broadcast_to(ai, xi.shape)
    _, _, hr, hi = lax.associative_scan(_complex_affine_combine, (a_r, a_i, xr, xi), axis=1)
    y = (jnp.einsum('blgn,gcn->blgc', hr, p['s5_c_re'].astype(F32))
         - jnp.einsum('blgn,gcn->blgc', hi, p['s5_c_im'].astype(F32))
         + p['s5_d'].astype(F32) * uf)
    g = jax.nn.gelu(y.reshape(b, l, MIX_D))
    ga, gb = jnp.split(g @ p['s5_glu_w'].astype(F32) + p['s5_glu_b'].astype(F32), 2, axis=-1)
    return (ga * jax.nn.sigmoid(gb)).astype(u.dtype), hr[:, -1], hi[:, -1]


def _hier_moe(h, p):
    t, d = h.shape
    lg = jnp.einsum('td,dg->tg', h, p['moe_w_group'], preferred_element_type=F32)
    grp = jnp.argmax(lg, axis=-1)
    tok = jnp.arange(t)
    p_grp = jax.nn.softmax(lg, axis=-1)[tok, grp]
    le = jnp.einsum('td,gde->tge', h, p['moe_w_expert'], preferred_element_type=F32)[tok, grp]
    top_l, top_i = lax.top_k(le, MOE_TOP_K)
    gate = jax.nn.softmax(top_l, axis=-1) * p_grp[:, None]
    e_flat = (grp[:, None] * MOE_EXPERTS_PER_GROUP + top_i).reshape(-1).astype(jnp.int32)
    n_assign = t * MOE_TOP_K
    order = jnp.argsort(e_flat)
    e_sorted = e_flat[order]
    counts = jnp.zeros((MOE_EXPERTS,), jnp.int32).at[e_flat].add(1)
    padded = (counts + MOE_BLOCK - 1) // MOE_BLOCK * MOE_BLOCK
    pad_end = jnp.cumsum(padded)
    pad_start = pad_end - padded
    start = jnp.cumsum(counts) - counts
    dest = pad_start[e_sorted] + jnp.arange(n_assign, dtype=jnp.int32) - start[e_sorted]
    n_blocks = -(-n_assign // MOE_BLOCK) + MOE_EXPERTS
    n_rows = n_blocks * MOE_BLOCK
    row_tok = jnp.full((n_rows,), t, jnp.int32).at[dest].set((order // MOE_TOP_K).astype(jnp.int32))
    row_w = jnp.zeros((n_rows,), F32).at[dest].set(gate.reshape(-1)[order])
    blk_expert = jnp.minimum(jnp.searchsorted(pad_end, jnp.arange(n_blocks, dtype=jnp.int32) * MOE_BLOCK,
                                              side='right'), MOE_EXPERTS - 1)
    hp = jnp.concatenate([h, jnp.zeros((1, d), h.dtype)], axis=0)

    def expert_block(args):
        rows, e = args
        xb = hp[rows]
        a = xb @ p['moe_w_gate'][e]
        v = xb @ p['moe_w_up'][e]
        return (jax.nn.silu(a) * v) @ p['moe_w_down'][e]

    out = lax.map(expert_block, (row_tok.reshape(n_blocks, MOE_BLOCK), blk_expert))
    out = out.reshape(n_rows, d)
    y = jnp.zeros((t + 1, d), out.dtype).at[row_tok].add(out * row_w[:, None].astype(out.dtype))
    return y[:t].astype(h.dtype)


def _layer(x, p, st, is_prompt):
    b, l, _ = x.shape
    xn = _rmsnorm(x, p['g_mix'])
    proj = xn @ p['w_in']
    u_a, u_b, u_c, u_d = jnp.split(proj, [SSD_IN, SSD_IN + CCM_IN, SSD_IN + CCM_IN + ATT_IN], axis=-1)
    if is_prompt:
        y_c, k_new, v_new = _band_attention_prompt(u_c, p['att_rel_bias'])
    else:
        y_c, k_new, v_new = _band_attention_sample(u_c, st['att_k'], st['att_v'], p['att_rel_bias'])
    y_a, ssd_conv, ssd = _ssd_mixer(u_a, st['ssd_conv'], st['ssd'], p)
    y_b, ccm_conv = _conv_module(u_b, st['ccm_conv'], p)
    y_d, s5_re, s5_im = _s5_mixer(u_d, st['s5_re'], st['s5_im'], p)
    mixed = jnp.concatenate([y_a.astype(x.dtype), y_b.astype(x.dtype), y_c.astype(x.dtype),
                             y_d.astype(x.dtype)], axis=-1) @ p['w_out']
    h = x + mixed.astype(x.dtype)
    hn = _rmsnorm(h, p['g_ffn']).reshape(b * l, D_MODEL)
    h = h + _hier_moe(hn, p).reshape(b, l, D_MODEL)
    new = {'ssd': ssd, 'ssd_conv': ssd_conv, 'ccm_conv': ccm_conv, 'att_k': k_new, 'att_v': v_new,
           's5_re': s5_re, 's5_im': s5_im}
    return h, new


def setup_inputs(seed: int = 0) -> dict:
    key = jax.random.key(seed)
    ks = iter(jax.random.split(key, 64))

    def nrm(shape, scale):
        return jax.random.normal(next(ks), shape, F32) * scale

    def uni(shape, lo, hi):
        return jax.random.uniform(next(ks), shape, F32, lo, hi)

    c_win = min(ATT_WINDOW, PAST_LEN)
    dt0 = jnp.exp(uni((DEPTH, SSD_HEADS), math.log(1e-3), math.log(1e-1)))
    return {
        'x_prompt': nrm((BATCH, SEQ, D_MODEL), 1.0),
        'x_sample': nrm((DEC_BATCH, DEC_SEQ, D_MODEL), 1.0),
        'state_ssd': nrm((DEPTH, DEC_BATCH, SSD_HEADS, SSD_HEAD_DIM, SSD_STATE), 0.5),
        'state_ssd_conv': nrm((DEPTH, DEC_BATCH, SSD_CONV - 1, SSD_CONV_CH), 1.0),
        'state_ccm_conv': nrm((DEPTH, DEC_BATCH, CCM_WIDTH - 1, MIX_B), 0.5),
        'cache_att_k': nrm((DEPTH, DEC_BATCH, c_win, ATT_HEADS, ATT_HEAD_DIM), 1.0),
        'cache_att_v': nrm((DEPTH, DEC_BATCH, c_win, ATT_HEADS, ATT_HEAD_DIM), 1.0),
        'state_s5_re': nrm((DEPTH, DEC_BATCH, S5_GROUPS, S5_STATE), 0.5),
        'state_s5_im': nrm((DEPTH, DEC_BATCH, S5_GROUPS, S5_STATE), 0.5),
        'g_mix': 1.0 + nrm((DEPTH, D_MODEL), 0.02),
        'w_in': nrm((DEPTH, D_MODEL, IN_WIDTH), D_MODEL ** -0.5),
        'ssd_conv_w': nrm((DEPTH, SSD_CONV, SSD_CONV_CH), SSD_CONV ** -0.5),
        'ssd_conv_b': nrm((DEPTH, SSD_CONV_CH), 0.02),
        'ssd_dt_bias': dt0 + jnp.log(-jnp.expm1(-dt0)),
        'ssd_a_log': jnp.log(uni((DEPTH, SSD_HEADS), 1.0, 16.0)),
        'ssd_d': 1.0 + nrm((DEPTH, SSD_HEADS), 0.1),
        'ssd_norm_w': 1.0 + nrm((DEPTH, MIX_A), 0.02),
        'ccm_dw_w': nrm((DEPTH, CCM_WIDTH, MIX_B), CCM_WIDTH ** -0.5),
        'ccm_dw_b': nrm((DEPTH, MIX_B), 0.02),
        'ccm_ln_g': 1.0 + nrm((DEPTH, MIX_B), 0.02),
        'ccm_ln_b': nrm((DEPTH, MIX_B), 0.02),
        'ccm_pw_w': nrm((DEPTH, MIX_B, MIX_B), MIX_B ** -0.5),
        'ccm_pw_b': nrm((DEPTH, MIX_B), 0.02),
        'att_rel_bias': nrm((DEPTH, ATT_HEADS, 2 * REL_MAX + 1), 0.1),
        's5_lam_re': -0.5 + nrm((DEPTH, S5_GROUPS, S5_STATE), 0.01),
        's5_lam_im': math.pi * jnp.arange(S5_STATE, dtype=F32) + nrm((DEPTH, S5_GROUPS, S5_STATE), 0.01),
        's5_log_dt': uni((DEPTH, S5_GROUPS), math.log(1e-3), math.log(1e-1)),
        's5_b_re': nrm((DEPTH, S5_GROUPS, S5_STATE, S5_GROUP_CH), (2 * S5_GROUP_CH) ** -0.5),
        's5_b_im': nrm((DEPTH, S5_GROUPS, S5_STATE, S5_GROUP_CH), (2 * S5_GROUP_CH) ** -0.5),
        's5_c_re': nrm((DEPTH, S5_GROUPS, S5_GROUP_CH, S5_STATE), (2 * S5_STATE) ** -0.5),
        's5_c_im': nrm((DEPTH, S5_GROUPS, S5_GROUP_CH, S5_STATE), (2 * S5_STATE) ** -0.5),
        's5_d': nrm((DEPTH, S5_GROUPS, S5_GROUP_CH), 0.5),
        's5_glu_w': nrm((DEPTH, MIX_D, 2 * MIX_D), MIX_D ** -0.5),
        's5_glu_b': nrm((DEPTH, 2 * MIX_D), 0.02),
        'w_out': nrm((DEPTH, MIX_WIDTH, D_MODEL), MIX_WIDTH ** -0.5),
        'g_ffn': 1.0 + nrm((DEPTH, D_MODEL), 0.02),
        'moe_w_group': nrm((DEPTH, D_MODEL, MOE_GROUPS), D_MODEL ** -0.5),
        'moe_w_expert': nrm((DEPTH, MOE_GROUPS, D_MODEL, MOE_EXPERTS_PER_GROUP), D_MODEL ** -0.5),
        'moe_w_gate': nrm((DEPTH, MOE_EXPERTS, D_MODEL, MOE_D_FF), D_MODEL ** -0.5),
        'moe_w_up': nrm((DEPTH, MOE_EXPERTS, D_MODEL, MOE_D_FF), D_MODEL ** -0.5),
        'moe_w_down': nrm((DEPTH, MOE_EXPERTS, MOE_D_FF, D_MODEL), MOE_D_FF ** -0.5),
        'g_final': 1.0 + nrm((D_MODEL,), 0.02),
    }


def reference(x_prompt, x_sample, state_ssd, state_ssd_conv, state_ccm_conv, cache_att_k, cache_att_v,
              state_s5_re, state_s5_im, g_mix, w_in, ssd_conv_w, ssd_conv_b, ssd_dt_bias, ssd_a_log, ssd_d,
              ssd_norm_w, ccm_dw_w, ccm_dw_b, ccm_ln_g, ccm_ln_b, ccm_pw_w, ccm_pw_b, att_rel_bias,
              s5_lam_re, s5_lam_im, s5_log_dt, s5_b_re, s5_b_im, s5_c_re, s5_c_im, s5_d, s5_glu_w, s5_glu_b,
              w_out, g_ffn, moe_w_group, moe_w_expert, moe_w_gate, moe_w_up, moe_w_down, g_final):
    names = ('ssd', 'ssd_conv', 'ccm_conv', 'att_k', 'att_v', 's5_re', 's5_im')
    new_p = {n: [] for n in names}
    new_s = {n: [] for n in names}
    bp = x_prompt.shape[0]
    yp, ys = x_prompt, x_sample
    for i in range(DEPTH):
        p = {'g_mix': g_mix[i], 'w_in': w_in[i], 'ssd_conv_w': ssd_conv_w[i], 'ssd_conv_b': ssd_conv_b[i],
             'ssd_dt_bias': ssd_dt_bias[i], 'ssd_a_log': ssd_a_log[i], 'ssd_d': ssd_d[i],
             'ssd_norm_w': ssd_norm_w[i], 'ccm_dw_w': ccm_dw_w[i], 'ccm_dw_b': ccm_dw_b[i],
             'ccm_ln_g': ccm_ln_g[i], 'ccm_ln_b': ccm_ln_b[i], 'ccm_pw_w': ccm_pw_w[i], 'ccm_pw_b': ccm_pw_b[i],
             'att_rel_bias': att_rel_bias[i], 's5_lam_re': s5_lam_re[i], 's5_lam_im': s5_lam_im[i],
             's5_log_dt': s5_log_dt[i], 's5_b_re': s5_b_re[i], 's5_b_im': s5_b_im[i], 's5_c_re': s5_c_re[i],
             's5_c_im': s5_c_im[i], 's5_d': s5_d[i], 's5_glu_w': s5_glu_w[i], 's5_glu_b': s5_glu_b[i],
             'w_out': w_out[i], 'g_ffn': g_ffn[i], 'moe_w_group': moe_w_group[i],
             'moe_w_expert': moe_w_expert[i], 'moe_w_gate': moe_w_gate[i], 'moe_w_up': moe_w_up[i],
             'moe_w_down': moe_w_down[i]}
        st_p = {'ssd': jnp.zeros((bp, SSD_HEADS, SSD_HEAD_DIM, SSD_STATE), F32),
                'ssd_conv': jnp.zeros((bp, SSD_CONV - 1, SSD_CONV_CH), x_prompt.dtype),
                'ccm_conv': jnp.zeros((bp, CCM_WIDTH - 1, MIX_B), x_prompt.dtype),
                's5_re': jnp.zeros((bp, S5_GROUPS, S5_STATE), F32),
                's5_im': jnp.zeros((bp, S5_GROUPS, S5_STATE), F32)}
        yp, sp = _layer(yp, p, st_p, True)
        st_s = {'ssd': state_ssd[i], 'ssd_conv': state_ssd_conv[i], 'ccm_conv': state_ccm_conv[i],
                'att_k': cache_att_k[i], 'att_v': cache_att_v[i], 's5_re': state_s5_re[i],
                's5_im': state_s5_im[i]}
        ys, ss = _layer(ys, p, st_s, False)
        for n in names:
            new_p[n].append(sp[n])
            new_s[n].append(ss[n])
    y_prompt = _rmsnorm(yp, g_final)
    y_sample = _rmsnorm(ys, g_final)
    sp = {n: jnp.stack(new_p[n]) for n in names}
    ss = {n: jnp.stack(new_s[n]) for n in names}
    return (y_prompt, y_sample, sp['ssd'], ss['ssd'], sp['ssd_conv'], ss['ssd_conv'], sp['ccm_conv'],
            ss['ccm_conv'], sp['att_k'], ss['att_k'], sp['att_v'], ss['att_v'], sp['s5_re'], ss['s5_re'],
            sp['s5_im'], ss['s5_im'])
```

```python
import functools
import math

import jax
import jax.numpy as jnp
from jax import lax
from jax.experimental import pallas as pl
from jax.experimental.pallas import tpu as pltpu

F32 = jnp.float32
BF16 = jnp.bfloat16

D_MODEL = 1024
CHUNK = 64
EPS = 1e-6
MIX = 256
SSD_HEADS = 4
SSD_HEAD_DIM = 64
SSD_STATE = 64
SSD_CONV = 4
SSD_CONV_CH = 512
CCM_WIDTH = 31
ATT_HEADS = 4
ATT_HEAD_DIM = 64
ATT_WINDOW = 512
ATT_BAND = ATT_WINDOW + CHUNK
ATT_SCALE = ATT_HEAD_DIM ** -0.5
REL_MAX = 128
S5_GROUPS = 16
S5_GROUP_CH = 16
S5_STATE = 64
S5_MODES = S5_GROUPS * S5_STATE
S5_STREAMS = 8
MOE_GROUPS = 4
MOE_EPG = 8
MOE_EXPERTS = 32
MOE_D_FF = 512
MOE_ROWS = 256
LANES = 128
ROUTE_LANE0 = MOE_GROUPS

COL_Z, COL_XBC, COL_CCM, COL_Q, COL_K, COL_V, COL_S5, COL_DT, COL_END = (
    0, 256, 768, 1280, 1536, 1792, 2048, 2304, 2432)

VMEM_LIMIT = 56 * 1024 * 1024


def _cparams(n_axes, vmem=VMEM_LIMIT):
    return pltpu.CompilerParams(dimension_semantics=("arbitrary",) * n_axes, vmem_limit_bytes=vmem)


def _dot(a, b):
    return jnp.dot(a, b, preferred_element_type=F32)


def _dot_nt(a, b):
    return lax.dot_general(a, b, (((1,), (1,)), ((), ())), preferred_element_type=F32)


def _dot_tn(a, b):
    return lax.dot_general(a, b, (((0,), (0,)), ((), ())), preferred_element_type=F32)


def _sigmoid(x):
    return 1.0 / (1.0 + jnp.exp(-x))


def _silu(x):
    return x * _sigmoid(x)


def _rms(x, g):
    return x * lax.rsqrt(jnp.mean(x * x, axis=-1, keepdims=True) + EPS) * g


def _in_proj_kernel(x_ref, g_ref, w_ref, z_ref, xbc_ref, dt_ref, ccm_ref, q_ref, k_ref, v_ref, u5_ref):
    xn = _rms(x_ref[0], g_ref[...]).astype(BF16)

    def mm(lo, hi):
        return _dot(xn, w_ref[:, lo:hi])

    z_ref[0] = mm(COL_Z, COL_XBC)
    xbc_ref[0] = mm(COL_XBC, COL_CCM)
    ccm_ref[0] = mm(COL_CCM, COL_Q)
    q_ref[0] = mm(COL_Q, COL_K) * ATT_SCALE
    k_ref[0] = mm(COL_K, COL_V)
    v_ref[0] = mm(COL_V, COL_S5)
    u5_ref[0] = mm(COL_S5, COL_DT)
    dt_ref[0] = mm(COL_DT, COL_END)


def _in_proj(x, g, w, tm):
    b, l, _ = x.shape
    nb = b // S5_STREAMS
    row = lambda width: pl.BlockSpec((1, tm, width), lambda i, j: (i, j, 0))
    outs = [
        jax.ShapeDtypeStruct((b, l, 256), F32),
        jax.ShapeDtypeStruct((b, l, 512), F32),
        jax.ShapeDtypeStruct((b, l, LANES), F32),
        jax.ShapeDtypeStruct((b, l, 512), F32),
        jax.ShapeDtypeStruct((b, l, 256), F32),
        jax.ShapeDtypeStruct((b, l, 256), F32),
        jax.ShapeDtypeStruct((b, l, 256), F32),
        jax.ShapeDtypeStruct((nb, l, S5_STREAMS * MIX), F32),
    ]
    out_specs = [row(256), row(512), row(LANES), row(512), row(256), row(256), row(256),
                 pl.BlockSpec((1, tm, MIX), lambda i, j: (i // S5_STREAMS, j, i % S5_STREAMS))]
    return pl.pallas_call(
        _in_proj_kernel,
        grid=(b, l // tm),
        in_specs=[row(D_MODEL),
                  pl.BlockSpec((1, D_MODEL), lambda i, j: (0, 0)),
                  pl.BlockSpec((D_MODEL, COL_END), lambda i, j: (0, 0))],
        out_specs=out_specs,
        out_shape=outs,
        compiler_params=_cparams(2),
        name="in_proj",
    )(x, g, w)


def _ssd_kernel(z_ref, xbc_ref, dt_ref, hist_ref, h0_ref, cw_ref, cb_ref, dtb_ref, alog_ref, dfull_ref,
                nw_ref, y_ref, hout_ref, xpad, hst, *, tl):
    j = pl.program_id(1)

    @pl.when(j == 0)
    def _():
        xpad[0:8, :] = hist_ref[0]
        hst[...] = h0_ref[0]

    @pl.when(j > 0)
    def _():
        xpad[0:8, :] = xpad[tl:tl + 8, :]

    xpad[8:8 + tl, :] = xbc_ref[0]
    conv = cb_ref[...]
    for kk in range(SSD_CONV):
        conv = conv + cw_ref[kk:kk + 1, :] * xpad[5 + kk:5 + kk + tl, :]
    xact = _silu(conv)

    raw = dt_ref[0] + dtb_ref[...]
    dt = jnp.maximum(raw, 0.0) + jnp.log(1.0 + jnp.exp(-jnp.abs(raw)))
    a_neg = -jnp.exp(alog_ref[...])
    acs = dt * a_neg
    row = lax.broadcasted_iota(jnp.int32, (tl, LANES), 0) & (CHUNK - 1)
    for s in (1, 2, 4, 8, 16, 32):
        acs = acs + jnp.where(row >= s, pltpu.roll(acs, s, axis=0), 0.0)
    acs_t = acs.T

    li = lax.broadcasted_iota(jnp.int32, (CHUNK, CHUNK), 0)
    si = lax.broadcasted_iota(jnp.int32, (CHUNK, CHUNK), 1)
    causal = li >= si
    zg = _silu(z_ref[0])
    for c in range(tl // CHUNK):
        r0 = c * CHUNK
        acs_c = acs[r0:r0 + CHUNK]
        acs_tc = acs_t[:, r0:r0 + CHUNK]
        dt_c = dt[r0:r0 + CHUNK]
        x_c = xact[r0:r0 + CHUNK]
        ys = []
        for g in range(2):
            bg = x_c[:, 256 + g * 64:256 + (g + 1) * 64].astype(BF16)
            cg = x_c[:, 384 + g * 64:384 + (g + 1) * 64].astype(BF16)
            cbm = _dot_nt(cg, bg)
            for h in (2 * g, 2 * g + 1):
                a_col = acs_c[:, h:h + 1]
                a_row = acs_tc[h:h + 1, :]
                lm = jnp.exp(jnp.where(causal, a_col - a_row, -jnp.inf))
                xs_h = x_c[:, h * 64:(h + 1) * 64]
                xd = xs_h * dt_c[:, h:h + 1]
                y_diag = _dot((cbm * lm).astype(BF16), xd.astype(BF16))
                a_last = acs_c[CHUNK - 1:CHUNK, h:h + 1]
                dec = jnp.exp(a_last - a_col)
                st = _dot_tn((xd * dec).astype(BF16), bg)
                h_prev = hst[h]
                y_off = jnp.exp(a_col) * _dot_nt(cg, h_prev.astype(BF16))
                hst[h] = jnp.exp(a_last) * h_prev + st
                ys.append(y_diag + y_off + dfull_ref[:, h * 64:(h + 1) * 64] * xs_h)
        y = jnp.concatenate(ys, axis=-1) * zg[r0:r0 + CHUNK]
        halves = []
        for g in range(2):
            yg = y[:, g * 128:(g + 1) * 128]
            halves.append(yg * lax.rsqrt(jnp.mean(yg * yg, axis=-1, keepdims=True) + EPS))
        y_ref[0, r0:r0 + CHUNK, :] = jnp.concatenate(halves, axis=-1) * nw_ref[...]

    @pl.when(j == pl.num_programs(1) - 1)
    def _():
        hout_ref[0] = hst[...]


def _ssd(z, xbc, dt, hist8, h0, cw, cb, dtb, alog, dfull, nw, tl):
    b, l, _ = z.shape
    row = lambda width: pl.BlockSpec((1, tl, width), lambda i, j: (i, j, 0))
    const = lambda shape: pl.BlockSpec(shape, lambda i, j: (0,) * len(shape))
    return pl.pallas_call(
        functools.partial(_ssd_kernel, tl=tl),
        grid=(b, l // tl),
        in_specs=[row(256), row(512), row(LANES),
                  pl.BlockSpec((1, 8, SSD_CONV_CH), lambda i, j: (i, 0, 0)),
                  pl.BlockSpec((1, SSD_HEADS, SSD_HEAD_DIM, SSD_STATE), lambda i, j: (i, 0, 0, 0)),
                  const((SSD_CONV, SSD_CONV_CH)), const((1, SSD_CONV_CH)), const((1, LANES)),
                  const((1, LANES)), const((1, MIX)), const((1, MIX))],
        out_specs=[row(256),
                   pl.BlockSpec((1, SSD_HEADS, SSD_HEAD_DIM, SSD_STATE), lambda i, j: (i, 0, 0, 0))],
        out_shape=[jax.ShapeDtypeStruct((b, l, MIX), F32),
                   jax.ShapeDtypeStruct((b, SSD_HEADS, SSD_HEAD_DIM, SSD_STATE), F32)],
        scratch_shapes=[pltpu.VMEM((8 + tl, SSD_CONV_CH), F32),
                        pltpu.VMEM((SSD_HEADS, SSD_HEAD_DIM, SSD_STATE), F32)],
        compiler_params=_cparams(2),
        name="ssd",
    )(z, xbc, dt, hist8, h0, cw, cb, dtb, alog, dfull, nw)


CCM_PAD = 32
CCM_SUB = 64


def _ccm_kernel(u_ref, hist_ref, w_ref, b_ref, lng_ref, lnb_ref, pw_ref, pwb_ref, y_ref, tail_ref, vpad, *, tl):
    j = pl.program_id(1)

    @pl.when(j == 0)
    def _():
        vpad[0:CCM_PAD, :] = hist_ref[0]

    @pl.when(j > 0)
    def _():
        vpad[0:CCM_PAD, :] = vpad[tl:tl + CCM_PAD, :]

    u = u_ref[0]
    vpad[CCM_PAD:CCM_PAD + tl, :] = u[:, :MIX] * _sigmoid(u[:, MIX:])
    off = CCM_PAD - (CCM_WIDTH - 1)
    for s in range(tl // CCM_SUB):
        r0 = s * CCM_SUB
        acc = jnp.broadcast_to(b_ref[...], (CCM_SUB, MIX))
        for kk in range(CCM_WIDTH):
            acc = acc + w_ref[kk:kk + 1, :] * vpad[r0 + off + kk:r0 + off + kk + CCM_SUB, :]
        mu = jnp.mean(acc, axis=-1, keepdims=True)
        cen = acc - mu
        var = jnp.mean(cen * cen, axis=-1, keepdims=True)
        v = _silu(cen * lax.rsqrt(var + EPS) * lng_ref[...] + lnb_ref[...])
        y_ref[0, r0:r0 + CCM_SUB, :] = _dot(v.astype(BF16), pw_ref[...]) + pwb_ref[...]

    @pl.when(j == pl.num_programs(1) - 1)
    def _():
        tail_ref[0] = vpad[tl:tl + CCM_PAD, :]


def _ccm(u, hist32, w32, bias, lng, lnb, pw, pwb, tl):
    b, l, _ = u.shape
    const = lambda shape: pl.BlockSpec(shape, lambda i, j: (0,) * len(shape))
    return pl.pallas_call(
        functools.partial(_ccm_kernel, tl=tl),
        grid=(b, l // tl),
        in_specs=[pl.BlockSpec((1, tl, 2 * MIX), lambda i, j: (i, j, 0)),
                  pl.BlockSpec((1, CCM_PAD, MIX), lambda i, j: (i, 0, 0)),
                  const((CCM_PAD, MIX)), const((1, MIX)), const((1, MIX)), const((1, MIX)),
                  const((MIX, MIX)), const((1, MIX))],
        out_specs=[pl.BlockSpec((1, tl, MIX), lambda i, j: (i, j, 0)),
                   pl.BlockSpec((1, CCM_PAD, MIX), lambda i, j: (i, 0, 0))],
        out_shape=[jax.ShapeDtypeStruct((b, l, MIX), F32),
                   jax.ShapeDtypeStruct((b, CCM_PAD, MIX), F32)],
        scratch_shapes=[pltpu.VMEM((CCM_PAD + tl, MIX), F32)],
        compiler_params=_cparams(2),
        name="ccm",
    )(u, hist32, w32, bias, lng, lnb, pw, pwb)


def _attn_kernel(*refs, seq, has_prefix):
    if has_prefix:
        q_ref, k_ref, v_ref, pk_ref, pv_ref, bias_ref, o_ref, kpad, vpad = refs
    else:
        q_ref, k_ref, v_ref, bias_ref, o_ref, kpad, vpad = refs
    c = pl.program_id(1)

    @pl.when(c == 0)
    def _():
        if has_prefix:
            kpad[0:ATT_WINDOW, :] = pk_ref[0].astype(BF16)
            vpad[0:ATT_WINDOW, :] = pv_ref[0].astype(BF16)
        else:
            kpad[0:ATT_WINDOW, :] = jnp.zeros((ATT_WINDOW, MIX), BF16)
            vpad[0:ATT_WINDOW, :] = jnp.zeros((ATT_WINDOW, MIX), BF16)
        kpad[ATT_WINDOW:ATT_WINDOW + seq, :] = k_ref[0].astype(BF16)
        vpad[ATT_WINDOW:ATT_WINDOW + seq, :] = v_ref[0].astype(BF16)

    start = pl.multiple_of(c * CHUNK, CHUNK)
    kwin = kpad[pl.ds(start, ATT_BAND), :]
    vwin = vpad[pl.ds(start, ATT_BAND), :]
    q = q_ref[0].astype(BF16)
    if not has_prefix:
        col = lax.broadcasted_iota(jnp.int32, (CHUNK, ATT_BAND), 1)
        valid = col >= (ATT_WINDOW - c * CHUNK)
    outs = []
    for h in range(ATT_HEADS):
        hs = slice(h * ATT_HEAD_DIM, (h + 1) * ATT_HEAD_DIM)
        s = _dot_nt(q[:, hs], kwin[:, hs]) + bias_ref[h]
        if not has_prefix:
            s = jnp.where(valid, s, -jnp.inf)
        m = jnp.max(s, axis=-1, keepdims=True)
        e = jnp.exp(s - m)
        p = e * (1.0 / jnp.sum(e, axis=-1, keepdims=True))
        outs.append(_dot(p.astype(BF16), vwin[:, hs]))
    o_ref[0] = jnp.concatenate(outs, axis=-1)


def _attn(q, k, v, bias, prefix=None):
    b, l, _ = q.shape
    has_prefix = prefix is not None
    full = pl.BlockSpec((1, l, MIX), lambda i, j: (i, 0, 0))
    in_specs = [pl.BlockSpec((1, CHUNK, MIX), lambda i, j: (i, j, 0)), full, full]
    args = [q, k, v]
    if has_prefix:
        pre = pl.BlockSpec((1, ATT_WINDOW, MIX), lambda i, j: (i, 0, 0))
        in_specs += [pre, pre]
        args += list(prefix)
    in_specs.append(pl.BlockSpec((ATT_HEADS, CHUNK, ATT_BAND), lambda i, j: (0, 0, 0)))
    args.append(bias)
    return pl.pallas_call(
        functools.partial(_attn_kernel, seq=l, has_prefix=has_prefix),
        grid=(b, l // CHUNK),
        in_specs=in_specs,
        out_specs=pl.BlockSpec((1, CHUNK, MIX), lambda i, j: (i, j, 0)),
        out_shape=jax.ShapeDtypeStruct((b, l, MIX), F32),
        scratch_shapes=[pltpu.VMEM((ATT_WINDOW + l, MIX), BF16),
                        pltpu.VMEM((ATT_WINDOW + l, MIX), BF16)],
        compiler_params=_cparams(2),
        name="attn",
    )(*args)


def _gelu_tanh(x):
    return 0.5 * x * (1.0 + jnp.tanh(math.sqrt(2.0 / math.pi) * (x + 0.044715 * (x * x * x))))


def _s5_kernel(u_ref, h0_ref, bb_ref, cc_ref, a_ref, d_ref, gw_ref, gb_ref, y_ref, hout_ref, xs, hst, *, tc):
    j = pl.program_id(1)

    @pl.when(j == 0)
    def _():
        hst[...] = h0_ref[0]

    u = u_ref[0]
    xs[...] = _dot(u.astype(BF16), bb_ref[...])
    ar = a_ref[:, :S5_MODES]
    ai = a_ref[:, S5_MODES:]

    def step(t, carry):
        hr, hi = carry
        r0 = pl.multiple_of(t * S5_STREAMS, S5_STREAMS)
        xr = xs[pl.ds(r0, S5_STREAMS), 0:S5_MODES]
        xi = xs[pl.ds(r0, S5_STREAMS), S5_MODES:2 * S5_MODES]
        nr = ar * hr - ai * hi + xr
        ni = ar * hi + ai * hr + xi
        xs[pl.ds(r0, S5_STREAMS), 0:S5_MODES] = nr
        xs[pl.ds(r0, S5_STREAMS), S5_MODES:2 * S5_MODES] = ni
        return nr, ni

    hr, hi = lax.fori_loop(0, tc, step, (hst[:, :S5_MODES], hst[:, S5_MODES:]))
    hst[:, :S5_MODES] = hr
    hst[:, S5_MODES:] = hi

    y = _dot(xs[...].astype(BF16), cc_ref[...]) + d_ref[...] * u
    gl = _dot(_gelu_tanh(y).astype(BF16), gw_ref[...]) + gb_ref[...]
    y_ref[0] = gl[:, :MIX] * _sigmoid(gl[:, MIX:])

    @pl.when(j == pl.num_programs(1) - 1)
    def _():
        hout_ref[0] = hst[...]


def _s5(u5, h0, bb, cc, a8, dvec, gw, gb, tc):
    nb, rows, _ = u5.shape
    blk = tc * S5_STREAMS
    const = lambda shape: pl.BlockSpec(shape, lambda i, j: (0,) * len(shape))
    return pl.pallas_call(
        functools.partial(_s5_kernel, tc=tc),
        grid=(nb, rows // blk),
        in_specs=[pl.BlockSpec((1, blk, MIX), lambda i, j: (i, j, 0)),
                  pl.BlockSpec((1, S5_STREAMS, 2 * S5_MODES), lambda i, j: (i, 0, 0)),
                  const((MIX, 2 * S5_MODES)), const((2 * S5_MODES, MIX)), const((S5_STREAMS, 2 * S5_MODES)),
                  const((1, MIX)), const((MIX, 2 * MIX)), const((1, 2 * MIX))],
        out_specs=[pl.BlockSpec((1, blk, MIX), lambda i, j: (i, j, 0)),
                   pl.BlockSpec((1, S5_STREAMS, 2 * S5_MODES), lambda i, j: (i, 0, 0))],
        out_shape=[jax.ShapeDtypeStruct((nb, rows, MIX), F32),
                   jax.ShapeDtypeStruct((nb, S5_STREAMS, 2 * S5_MODES), F32)],
        scratch_shapes=[pltpu.VMEM((blk, 2 * S5_MODES), F32),
                        pltpu.VMEM((S5_STREAMS, 2 * S5_MODES), F32)],
        compiler_params=_cparams(2),
        name="s5",
    )(u5, h0, bb, cc, a8, dvec, gw, gb)


def _out_router_kernel(x_ref, ya_ref, yb_ref, yc_ref, yd_ref, wout_ref, g_ref, wr_ref, tril_ref,
                       h_ref, hn_ref, route_ref, cnt_ref, base, *, tm):
    first = jnp.logical_and(pl.program_id(0) == 0, pl.program_id(1) == 0)

    @pl.when(first)
    def _():
        base[...] = jnp.zeros_like(base)

    mix = jnp.concatenate([ya_ref[0], yb_ref[0], yc_ref[0], yd_ref[0]], axis=-1).astype(BF16)
    h = x_ref[0] + _dot(mix, wout_ref[...])
    h_ref[0] = h
    hn = _rms(h, g_ref[...])
    hn_ref[0] = hn
    lg = _dot(hn.astype(BF16), wr_ref[...])

    lane = lax.broadcasted_iota(jnp.int32, (tm, LANES), 1).astype(F32)
    big = float(LANES)
    is_grp = lane < MOE_GROUPS
    lgg = jnp.where(is_grp, lg, -jnp.inf)
    mg = jnp.max(lgg, axis=-1, keepdims=True)
    grp = jnp.min(jnp.where(lgg == mg, lane, big), axis=-1, keepdims=True)
    p_grp = 1.0 / jnp.sum(jnp.where(is_grp, jnp.exp(lgg - mg), 0.0), axis=-1, keepdims=True)
    lo = ROUTE_LANE0 + MOE_EPG * grp
    in_grp = jnp.logical_and(lane >= lo, lane < lo + MOE_EPG)
    le = jnp.where(in_grp, lg, -jnp.inf)
    t1 = jnp.max(le, axis=-1, keepdims=True)
    i1 = jnp.min(jnp.where(le == t1, lane, big), axis=-1, keepdims=True)
    le2 = jnp.where(lane == i1, -jnp.inf, le)
    t2 = jnp.max(le2, axis=-1, keepdims=True)
    i2 = jnp.min(jnp.where(le2 == t2, lane, big), axis=-1, keepdims=True)
    e21 = jnp.exp(t2 - t1)
    inv = 1.0 / (1.0 + e21)
    g1 = inv * p_grp
    g2 = (e21 * inv) * p_grp

    oh1 = lane == i1
    oh2 = lane == i2
    ohs = jnp.where(jnp.logical_or(oh1, oh2), 1.0, 0.0)
    cnt = _dot(tril_ref[...], ohs.astype(BF16)) + base[...]
    r1 = jnp.sum(jnp.where(oh1, cnt, 0.0), axis=-1, keepdims=True)
    r2 = jnp.sum(jnp.where(oh2, cnt, 0.0), axis=-1, keepdims=True)
    base[...] = base[...] + jnp.sum(ohs, axis=0, keepdims=True)

    route = jnp.where(lane == 0.0, i1 - ROUTE_LANE0, 0.0)
    route = jnp.where(lane == 1.0, i2 - ROUTE_LANE0, route)
    route = jnp.where(lane == 2.0, r1, route)
    route = jnp.where(lane == 3.0, r2, route)
    route = jnp.where(lane == 4.0, g1, route)
    route = jnp.where(lane == 5.0, g2, route)
    route_ref[0] = route
    cnt_ref[...] = jnp.broadcast_to(base[...], (8, LANES))


def _out_router(x, ya, yb, yc, yd5, wout, g, wr, tril, tm):
    b, l, _ = x.shape
    row = lambda width: pl.BlockSpec((1, tm, width), lambda i, j: (i, j, 0))
    const = lambda shape: pl.BlockSpec(shape, lambda i, j: (0,) * len(shape))
    return pl.pallas_call(
        functools.partial(_out_router_kernel, tm=tm),
        grid=(b, l // tm),
        in_specs=[row(D_MODEL), row(MIX), row(MIX), row(MIX),
                  pl.BlockSpec((1, tm, MIX), lambda i, j: (i // S5_STREAMS, j, i % S5_STREAMS)),
                  const((D_MODEL, D_MODEL)), const((1, D_MODEL)), const((D_MODEL, LANES)), const((tm, tm))],
        out_specs=[row(D_MODEL), row(D_MODEL), row(LANES), const((8, LANES))],
        out_shape=[jax.ShapeDtypeStruct((b, l, D_MODEL), F32),
                   jax.ShapeDtypeStruct((b, l, D_MODEL), F32),
                   jax.ShapeDtypeStruct((b, l, LANES), F32),
                   jax.ShapeDtypeStruct((8, LANES), F32)],
        scratch_shapes=[pltpu.VMEM((1, LANES), F32)],
        compiler_params=_cparams(2),
        name="out_router",
    )(x, ya, yb, yc, yd5, wout, g, wr, tril)


def _row_copy(src, src_row, dst, dst_row, sem):
    return pltpu.make_async_copy(src.at[pl.ds(src_row, 1), :], dst.at[pl.ds(dst_row, 1), :], sem)


def _dispatch_kernel(dest_ref, hn_ref, xs_in_ref, xs_ref, sem, *, tm):
    del xs_in_ref

    def issue(i, carry):
        for kk in range(2):
            _row_copy(hn_ref, i, xs_ref, dest_ref[0, 0, 2 * i + kk], sem).start()
        return carry

    lax.fori_loop(0, tm, issue, 0)

    def drain(i, carry):
        for kk in range(2):
            _row_copy(hn_ref, 0, xs_ref, 0, sem).wait()
        return carry

    lax.fori_loop(0, tm, drain, 0)


def _dispatch(dest3, hn, xs0, tm):
    t = hn.shape[0]
    return pl.pallas_call(
        functools.partial(_dispatch_kernel, tm=tm),
        grid=(t // tm,),
        in_specs=[pl.BlockSpec((1, 1, 2 * tm), lambda i: (i, 0, 0), memory_space=pltpu.SMEM),
                  pl.BlockSpec((tm, D_MODEL), lambda i: (i, 0)),
                  pl.BlockSpec(memory_space=pl.ANY)],
        out_specs=pl.BlockSpec(memory_space=pl.ANY),
        out_shape=jax.ShapeDtypeStruct(xs0.shape, F32),
        scratch_shapes=[pltpu.SemaphoreType.DMA(())],
        input_output_aliases={2: 0},
        compiler_params=_cparams(1),
        name="dispatch",
    )(dest3, hn, xs0)


def _ffn_kernel(be_ref, xs_ref, wg_ref, wu_ref, wd_ref, ys_ref):
    del be_ref
    xb = xs_ref[...].astype(BF16)
    a = _dot(xb, wg_ref[0])
    v = _dot(xb, wu_ref[0])
    ys_ref[...] = _dot((_silu(a) * v).astype(BF16), wd_ref[0])


def _ffn(blk_expert, xs, wg, wu, wd):
    n_rows = xs.shape[0]
    grid_spec = pltpu.PrefetchScalarGridSpec(
        num_scalar_prefetch=1,
        grid=(n_rows // MOE_ROWS,),
        in_specs=[pl.BlockSpec((MOE_ROWS, D_MODEL), lambda i, be: (i, 0)),
                  pl.BlockSpec((1, D_MODEL, MOE_D_FF), lambda i, be: (be[i], 0, 0)),
                  pl.BlockSpec((1, D_MODEL, MOE_D_FF), lambda i, be: (be[i], 0, 0)),
                  pl.BlockSpec((1, MOE_D_FF, D_MODEL), lambda i, be: (be[i], 0, 0))],
        out_specs=pl.BlockSpec((MOE_ROWS, D_MODEL), lambda i, be: (i, 0)),
    )
    return pl.pallas_call(
        _ffn_kernel,
        grid_spec=grid_spec,
        out_shape=jax.ShapeDtypeStruct((n_rows, D_MODEL), F32),
        compiler_params=_cparams(1),
        name="expert_ffn",
    )(blk_expert, xs, wg, wu, wd)


def _combine_kernel(dest_ref, h_ref, route_ref, gfin_ref, ys_ref, out_ref, buf0, buf1, sem, *, tm, final):
    def issue(i, carry):
        _row_copy(ys_ref, dest_ref[0, 0, 2 * i], buf0, i, sem).start()
        _row_copy(ys_ref, dest_ref[0, 0, 2 * i + 1], buf1, i, sem).start()
        return carry

    lax.fori_loop(0, tm, issue, 0)

    def drain(i, carry):
        _row_copy(ys_ref, 0, buf0, 0, sem).wait()
        _row_copy(ys_ref, 0, buf1, 0, sem).wait()
        return carry

    lax.fori_loop(0, tm, drain, 0)
    route = route_ref[...]
    h = h_ref[...] + (route[:, 4:5] * buf0[...] + route[:, 5:6] * buf1[...])
    out_ref[...] = _rms(h, gfin_ref[...]) if final else h


def _combine(dest3, h, route, gfin, ys, tm, final):
    t = h.shape[0]
    return pl.pallas_call(
        functools.partial(_combine_kernel, tm=tm, final=final),
        grid=(t // tm,),
        in_specs=[pl.BlockSpec((1, 1, 2 * tm), lambda i: (i, 0, 0), memory_space=pltpu.SMEM),
                  pl.BlockSpec((tm, D_MODEL), lambda i: (i, 0)),
                  pl.BlockSpec((tm, LANES), lambda i: (i, 0)),
                  pl.BlockSpec((1, D_MODEL), lambda i: (0, 0)),
                  pl.BlockSpec(memory_space=pl.ANY)],
        out_specs=pl.BlockSpec((tm, D_MODEL), lambda i: (i, 0)),
        out_shape=jax.ShapeDtypeStruct((t, D_MODEL), F32),
        scratch_shapes=[pltpu.VMEM((tm, D_MODEL), F32), pltpu.VMEM((tm, D_MODEL), F32),
                        pltpu.SemaphoreType.DMA(())],
        compiler_params=_cparams(1),
        name="combine",
    )(dest3, h, route, gfin, ys)


def _prep_layer(p):
    w_in = p["w_in"]
    z, xbc, dt, ccm, q, k, v, s5 = jnp.split(
        w_in, [256, 768, 772, 1284, 1540, 1796, 2052], axis=1)
    dt = jnp.pad(dt, ((0, 0), (0, LANES - SSD_HEADS)))
    w_in_p = jnp.concatenate([z, xbc, ccm, q, k, v, s5, dt], axis=1).astype(BF16)

    pad_l = lambda a: jnp.pad(a, (0, LANES - a.shape[0]))[None, :]
    d = ATT_WINDOW + jnp.arange(CHUNK)[:, None] - jnp.arange(ATT_BAND)[None, :]
    bias = p["att_rel_bias"].astype(F32)[:, jnp.clip(d, -REL_MAX, REL_MAX) + REL_MAX]

    lr, li = p["s5_lam_re"], p["s5_lam_im"]
    dts = jnp.exp(p["s5_log_dt"])[:, None]
    mag = jnp.exp(lr * dts)
    ar, ai = mag * jnp.cos(li * dts), mag * jnp.sin(li * dts)
    den = lr * lr + li * li
    cr = ((ar - 1.0) * lr + ai * li) / den
    ci = (ai * lr - (ar - 1.0) * li) / den
    br, bi = p["s5_b_re"], p["s5_b_im"]
    bbr = cr[..., None] * br - ci[..., None] * bi
    bbi = cr[..., None] * bi + ci[..., None] * br
    eye = jnp.eye(S5_GROUPS, dtype=F32)
    drive = lambda m: jnp.einsum("gnc,gk->gckn", m, eye).reshape(MIX, S5_MODES)
    bb = jnp.concatenate([drive(bbr), drive(bbi)], axis=1).astype(BF16)
    read = lambda m: jnp.einsum("gcn,gk->gnkc", m, eye).reshape(S5_MODES, MIX)
    cc = jnp.concatenate([read(p["s5_c_re"]), -read(p["s5_c_im"])], axis=0).astype(BF16)
    a8 = jnp.broadcast_to(jnp.concatenate([ar.reshape(-1), ai.reshape(-1)])[None, :],
                          (S5_STREAMS, 2 * S5_MODES))

    w_route = jnp.concatenate(
        [p["moe_w_group"], jnp.transpose(p["moe_w_expert"], (1, 0, 2)).reshape(D_MODEL, MOE_EXPERTS)], axis=1)
    w_route = jnp.pad(w_route, ((0, 0), (0, LANES - w_route.shape[1]))).astype(BF16)
    return dict(
        g_mix=p["g_mix"][None, :], w_in=w_in_p,
        ssd_cw=p["ssd_conv_w"], ssd_cb=p["ssd_conv_b"][None, :], ssd_dtb=pad_l(p["ssd_dt_bias"]),
        ssd_alog=pad_l(p["ssd_a_log"]), ssd_dfull=jnp.repeat(p["ssd_d"], SSD_HEAD_DIM)[None, :],
        ssd_nw=p["ssd_norm_w"][None, :],
        ccm_w=jnp.pad(p["ccm_dw_w"], ((0, CCM_PAD - CCM_WIDTH), (0, 0))), ccm_b=p["ccm_dw_b"][None, :],
        ccm_lng=p["ccm_ln_g"][None, :], ccm_lnb=p["ccm_ln_b"][None, :],
        ccm_pw=p["ccm_pw_w"].astype(BF16), ccm_pwb=p["ccm_pw_b"][None, :],
        att_bias=bias,
        s5_bb=bb, s5_cc=cc, s5_a8=a8, s5_d=p["s5_d"].reshape(1, MIX),
        s5_gw=p["s5_glu_w"].astype(BF16), s5_gb=p["s5_glu_b"][None, :],
        w_out=p["w_out"].astype(BF16), g_ffn=p["g_ffn"][None, :], w_route=w_route,
        moe_wg=p["moe_w_gate"].astype(BF16), moe_wu=p["moe_w_up"].astype(BF16),
        moe_wd=p["moe_w_down"].astype(BF16),
    )


def _tile(n, pref):
    return pref if n % pref == 0 else n


def _layer(x, w, st, g_final, final):
    b, l, _ = x.shape
    t = b * l
    nb = b // S5_STREAMS
    tm = _tile(l, 512)
    z, xbc, dt, ccm, q, k, v, u5 = _in_proj(x, w["g_mix"], w["w_in"], tm)

    tl = _tile(l, 256)
    hist8 = jnp.pad(st["ssd_conv"], ((0, 0), (8 - (SSD_CONV - 1), 0), (0, 0)))
    y_a, ssd_new = _ssd(z, xbc, dt, hist8, st["ssd"], w["ssd_cw"], w["ssd_cb"], w["ssd_dtb"],
                        w["ssd_alog"], w["ssd_dfull"], w["ssd_nw"], tl)
    ssd_conv_new = jnp.concatenate([st["ssd_conv"], xbc], axis=1)[:, -(SSD_CONV - 1):]

    hist32 = jnp.pad(st["ccm_conv"], ((0, 0), (CCM_PAD - (CCM_WIDTH - 1), 0), (0, 0)))
    y_b, ccm_tail = _ccm(ccm, hist32, w["ccm_w"], w["ccm_b"], w["ccm_lng"], w["ccm_lnb"],
                         w["ccm_pw"], w["ccm_pwb"], tl)
    ccm_conv_new = ccm_tail[:, CCM_PAD - (CCM_WIDTH - 1):]

    if "att_k" in st:
        prefix = (st["att_k"].reshape(b, ATT_WINDOW, MIX), st["att_v"].reshape(b, ATT_WINDOW, MIX))
        y_c = _attn(q, k, v, w["att_bias"], prefix)
    else:
        y_c = _attn(q, k, v, w["att_bias"])
    keep = min(ATT_WINDOW, l)
    k_new = k[:, l - keep:].reshape(b, keep, ATT_HEADS, ATT_HEAD_DIM)
    v_new = v[:, l - keep:].reshape(b, keep, ATT_HEADS, ATT_HEAD_DIM)

    h0 = jnp.concatenate([st["s5_re"].reshape(nb, S5_STREAMS, S5_MODES),
                          st["s5_im"].reshape(nb, S5_STREAMS, S5_MODES)], axis=-1)
    y_d5, s5_new = _s5(u5.reshape(nb, l * S5_STREAMS, MIX), h0, w["s5_bb"], w["s5_cc"], w["s5_a8"],
                       w["s5_d"], w["s5_gw"], w["s5_gb"], _tile(l, 64))
    s5_re_new = s5_new[:, :, :S5_MODES].reshape(b, S5_GROUPS, S5_STATE)
    s5_im_new = s5_new[:, :, S5_MODES:].reshape(b, S5_GROUPS, S5_STATE)

    tril = jnp.tril(jnp.ones((tm, tm), BF16), -1)
    h, hn, route, cnt = _out_router(x, y_a, y_b, y_c, y_d5.reshape(nb, l, S5_STREAMS * MIX),
                                    w["w_out"], w["g_ffn"], w["w_route"], tril, tm)

    route = route.reshape(t, LANES)
    counts = cnt[0, ROUTE_LANE0:ROUTE_LANE0 + MOE_EXPERTS].astype(jnp.int32)
    padded = (counts + MOE_ROWS - 1) // MOE_ROWS * MOE_ROWS
    pad_end = jnp.cumsum(padded)
    pad_start = pad_end - padded
    n_blocks = -(-(2 * t) // MOE_ROWS) + MOE_EXPERTS
    dest = pad_start[route[:, 0:2].astype(jnp.int32)] + route[:, 2:4].astype(jnp.int32)
    blk_expert = jnp.minimum(
        jnp.searchsorted(pad_end, jnp.arange(n_blocks, dtype=jnp.int32) * MOE_ROWS, side="right"),
        MOE_EXPERTS - 1).astype(jnp.int32)

    td = _tile(t, 256)
    dest3 = dest.reshape(t // td, 1, 2 * td)
    xs = _dispatch(dest3, hn.reshape(t, D_MODEL), jnp.zeros((n_blocks * MOE_ROWS, D_MODEL), F32), td)
    ys = _ffn(blk_expert, xs, w["moe_wg"], w["moe_wu"], w["moe_wd"])
    out = _combine(dest3, h.reshape(t, D_MODEL), route, g_final, ys, td, final).reshape(b, l, D_MODEL)

    new = dict(ssd=ssd_new, ssd_conv=ssd_conv_new, ccm_conv=ccm_conv_new, att_k=k_new, att_v=v_new,
               s5_re=s5_re_new, s5_im=s5_im_new)
    return out, new


NAMES = ("ssd", "ssd_conv", "ccm_conv", "att_k", "att_v", "s5_re", "s5_im")


def kernel(x_prompt, x_sample, state_ssd, state_ssd_conv, state_ccm_conv, cache_att_k, cache_att_v, state_s5_re, state_s5_im, g_mix, w_in, ssd_conv_w, ssd_conv_b, ssd_dt_bias, ssd_a_log, ssd_d, ssd_norm_w, ccm_dw_w, ccm_dw_b, ccm_ln_g, ccm_ln_b, ccm_pw_w, ccm_pw_b, att_rel_bias, s5_lam_re, s5_lam_im, s5_log_dt, s5_b_re, s5_b_im, s5_c_re, s5_c_im, s5_d, s5_glu_w, s5_glu_b, w_out, g_ffn, moe_w_group, moe_w_expert, moe_w_gate, moe_w_up, moe_w_down, g_final):
    params = dict(g_mix=g_mix, w_in=w_in, ssd_conv_w=ssd_conv_w, ssd_conv_b=ssd_conv_b,
                  ssd_dt_bias=ssd_dt_bias, ssd_a_log=ssd_a_log, ssd_d=ssd_d, ssd_norm_w=ssd_norm_w,
                  ccm_dw_w=ccm_dw_w, ccm_dw_b=ccm_dw_b, ccm_ln_g=ccm_ln_g, ccm_ln_b=ccm_ln_b,
                  ccm_pw_w=ccm_pw_w, ccm_pw_b=ccm_pw_b, att_rel_bias=att_rel_bias, s5_lam_re=s5_lam_re,
                  s5_lam_im=s5_lam_im, s5_log_dt=s5_log_dt, s5_b_re=s5_b_re, s5_b_im=s5_b_im,
                  s5_c_re=s5_c_re, s5_c_im=s5_c_im, s5_d=s5_d, s5_glu_w=s5_glu_w, s5_glu_b=s5_glu_b,
                  w_out=w_out, g_ffn=g_ffn, moe_w_group=moe_w_group, moe_w_expert=moe_w_expert,
                  moe_w_gate=moe_w_gate, moe_w_up=moe_w_up, moe_w_down=moe_w_down)
    depth = w_in.shape[0]
    bp = x_prompt.shape[0]
    g_fin = g_final[None, :]
    new_p = {n: [] for n in NAMES}
    new_s = {n: [] for n in NAMES}
    yp, ys = x_prompt, x_sample
    for i in range(depth):
        w = _prep_layer({n: a[i] for n, a in params.items()})
        final = i == depth - 1
        st_p = dict(ssd=jnp.zeros((bp, SSD_HEADS, SSD_HEAD_DIM, SSD_STATE), F32),
                    ssd_conv=jnp.zeros((bp, SSD_CONV - 1, SSD_CONV_CH), F32),
                    ccm_conv=jnp.zeros((bp, CCM_WIDTH - 1, MIX), F32),
                    s5_re=jnp.zeros((bp, S5_GROUPS, S5_STATE), F32),
                    s5_im=jnp.zeros((bp, S5_GROUPS, S5_STATE), F32))
        yp, sp = _layer(yp, w, st_p, g_fin, final)
        st_s = dict(ssd=state_ssd[i], ssd_conv=state_ssd_conv[i], ccm_conv=state_ccm_conv[i],
                    att_k=cache_att_k[i], att_v=cache_att_v[i], s5_re=state_s5_re[i], s5_im=state_s5_im[i])
        ys, ss = _layer(ys, w, st_s, g_fin, final)
        for n in NAMES:
            new_p[n].append(sp[n])
            new_s[n].append(ss[n])
    sp = {n: jnp.stack(new_p[n]) for n in NAMES}
    ss = {n: jnp.stack(new_s[n]) for n in NAMES}
    return (yp, ys, sp["ssd"], ss["ssd"], sp["ssd_conv"], ss["ssd_conv"], sp["ccm_conv"], ss["ccm_conv"],
            sp["att_k"], ss["att_k"], sp["att_v"], ss["att_v"], sp["s5_re"], ss["s5_re"],
            sp["s5_im"], ss["s5_im"])
```

```python
import functools
import math

import jax
import jax.numpy as jnp
from jax import lax
from jax.experimental import pallas as pl
from jax.experimental.pallas import tpu as pltpu

F32 = jnp.float32
BF16 = jnp.bfloat16

D_MODEL = 1024
CHUNK = 64
EPS = 1e-6
MIX = 256
SSD_HEADS = 4
SSD_HEAD_DIM = 64
SSD_STATE = 64
SSD_CONV = 4
SSD_CONV_CH = 512
CCM_WIDTH = 31
ATT_HEADS = 4
ATT_HEAD_DIM = 64
ATT_WINDOW = 512
ATT_BAND = ATT_WINDOW + CHUNK
ATT_SCALE = ATT_HEAD_DIM ** -0.5
REL_MAX = 128
S5_GROUPS = 16
S5_GROUP_CH = 16
S5_STATE = 64
S5_MODES = S5_GROUPS * S5_STATE
S5_STREAMS = 8
MOE_GROUPS = 4
MOE_EPG = 8
MOE_EXPERTS = 32
MOE_D_FF = 512
MOE_ROWS = 512
DMA_UNROLL = 8
LANES = 128
ROUTE_LANE0 = MOE_GROUPS

COL_Z, COL_XBC, COL_CCM, COL_Q, COL_K, COL_V, COL_S5, COL_DT, COL_END = (
    0, 256, 768, 1280, 1536, 1792, 2048, 2304, 2432)

VMEM_LIMIT = 56 * 1024 * 1024


def _cparams(n_axes, vmem=VMEM_LIMIT):
    return pltpu.CompilerParams(dimension_semantics=("arbitrary",) * n_axes, vmem_limit_bytes=vmem)


def _dot(a, b):
    return jnp.dot(a, b, preferred_element_type=F32)


def _dot_nt(a, b):
    return lax.dot_general(a, b, (((1,), (1,)), ((), ())), preferred_element_type=F32)


def _dot_tn(a, b):
    return lax.dot_general(a, b, (((0,), (0,)), ((), ())), preferred_element_type=F32)


def _sigmoid(x):
    return 1.0 / (1.0 + jnp.exp(-x))


def _silu(x):
    return x * _sigmoid(x)


def _round_bf16(x):
    return x.astype(BF16).astype(F32)


def _rms(x, g):
    return x * lax.rsqrt(jnp.mean(x * x, axis=-1, keepdims=True) + EPS) * g


def _in_proj_kernel(x_ref, g_ref, w_ref, z_ref, xbc_ref, dt_ref, ccm_ref, q_ref, k_ref, v_ref, u5_ref):
    xn = _rms(x_ref[0], g_ref[...]).astype(BF16)

    def mm(lo, hi):
        return _dot(xn, w_ref[:, lo:hi])

    z_ref[0] = mm(COL_Z, COL_XBC)
    xbc_ref[0] = mm(COL_XBC, COL_CCM)
    ccm_ref[0] = mm(COL_CCM, COL_Q)
    q_ref[0] = mm(COL_Q, COL_K) * ATT_SCALE
    k_ref[0] = mm(COL_K, COL_V)
    v_ref[0] = mm(COL_V, COL_S5)
    u5_ref[0] = mm(COL_S5, COL_DT)
    dt_ref[0] = mm(COL_DT, COL_END)


def _in_proj(x, g, w, tm):
    b, l, _ = x.shape
    nb = b // S5_STREAMS
    row = lambda width: pl.BlockSpec((1, tm, width), lambda i, j: (i, j, 0))
    outs = [
        jax.ShapeDtypeStruct((b, l, 256), F32),
        jax.ShapeDtypeStruct((b, l, 512), F32),
        jax.ShapeDtypeStruct((b, l, LANES), F32),
        jax.ShapeDtypeStruct((b, l, 512), F32),
        jax.ShapeDtypeStruct((b, l, 256), F32),
        jax.ShapeDtypeStruct((b, l, 256), F32),
        jax.ShapeDtypeStruct((b, l, 256), F32),
        jax.ShapeDtypeStruct((nb, l, S5_STREAMS * MIX), F32),
    ]
    out_specs = [row(256), row(512), row(LANES), row(512), row(256), row(256), row(256),
                 pl.BlockSpec((1, tm, MIX), lambda i, j: (i // S5_STREAMS, j, i % S5_STREAMS))]
    return pl.pallas_call(
        _in_proj_kernel,
        grid=(b, l // tm),
        in_specs=[row(D_MODEL),
                  pl.BlockSpec((1, D_MODEL), lambda i, j: (0, 0)),
                  pl.BlockSpec((D_MODEL, COL_END), lambda i, j: (0, 0))],
        out_specs=out_specs,
        out_shape=outs,
        compiler_params=_cparams(2),
        name="in_proj",
    )(x, g, w)


def _ssd_kernel(z_ref, xbc_ref, dt_ref, hist_ref, h0_ref, cw_ref, cb_ref, dtb_ref, alog_ref, dfull_ref,
                nw_ref, y_ref, hout_ref, xpad, sb, segs, wbuf, *, tl):
    j = pl.program_id(1)
    nzp = segs.shape[0]

    @pl.when(j == 0)
    def _():
        xpad[0:8, :] = _round_bf16(hist_ref[0])
        sb[0] = _round_bf16(h0_ref[0])
        segs[...] = jnp.zeros_like(segs)

    @pl.when(j > 0)
    def _():
        xpad[0:8, :] = xpad[tl:tl + 8, :]

    xpad[8:8 + tl, :] = _round_bf16(xbc_ref[0])
    zrow = lax.broadcasted_iota(jnp.int32, (nzp, LANES), 0)

    def entering_state(n_states):
        wbuf[...] = _round_bf16(jnp.exp(segs[...]))

        def add(zi, hs):
            wrow = wbuf[pl.ds(zi, 1), :]
            return tuple(hs[h] + wrow[:, h:h + 1] * sb[zi, h] for h in range(SSD_HEADS))

        zero = jnp.zeros((SSD_HEAD_DIM, SSD_STATE), F32)
        return lax.fori_loop(0, n_states, add, (zero,) * SSD_HEADS)

    taps = _round_bf16(cw_ref[...])
    conv = taps[0:1, :] * xpad[5:5 + tl, :]
    for kk in range(1, SSD_CONV):
        conv = conv + taps[kk:kk + 1, :] * xpad[5 + kk:5 + kk + tl, :]
    xact = _silu(conv + cb_ref[...])

    raw = dt_ref[0] + dtb_ref[...]
    dt = jnp.maximum(raw, 0.0) + jnp.log(1.0 + jnp.exp(-jnp.abs(raw)))
    a_neg = -jnp.exp(alog_ref[...])
    acs = dt * a_neg
    row = lax.broadcasted_iota(jnp.int32, (tl, LANES), 0) & (CHUNK - 1)
    for s in (1, 2, 4, 8, 16, 32):
        acs = acs + jnp.where(row >= s, pltpu.roll(acs, s, axis=0), 0.0)
    acs_t = acs.T

    li = lax.broadcasted_iota(jnp.int32, (CHUNK, CHUNK), 0)
    si = lax.broadcasted_iota(jnp.int32, (CHUNK, CHUNK), 1)
    causal = li >= si
    zg = _silu(z_ref[0])
    for c in range(tl // CHUNK):
        r0 = c * CHUNK
        cidx = j * (tl // CHUNK) + c
        h_in = entering_state(cidx + 1)
        acs_c = acs[r0:r0 + CHUNK]
        acs_tc = acs_t[:, r0:r0 + CHUNK]
        dt_c = dt[r0:r0 + CHUNK]
        x_c = xact[r0:r0 + CHUNK]
        ys = []
        for g in range(2):
            bg = x_c[:, 256 + g * 64:256 + (g + 1) * 64].astype(BF16)
            cg = x_c[:, 384 + g * 64:384 + (g + 1) * 64].astype(BF16)
            cbm = _dot_nt(cg, bg)
            for h in (2 * g, 2 * g + 1):
                a_col = acs_c[:, h:h + 1]
                a_row = acs_tc[h:h + 1, :]
                lm = jnp.exp(jnp.where(causal, a_col - a_row, -jnp.inf))
                xs_h = x_c[:, h * 64:(h + 1) * 64]
                xd = xs_h * dt_c[:, h:h + 1]
                y_diag = _dot((cbm * lm).astype(BF16), xd.astype(BF16))
                a_last = acs_c[CHUNK - 1:CHUNK, h:h + 1]
                dec = jnp.exp(a_last - a_col)
                sb[cidx + 1, h] = _round_bf16(_dot_tn((xd * dec).astype(BF16), bg))
                y_off = jnp.exp(a_col) * _dot_nt(cg, h_in[h].astype(BF16))
                ys.append(y_diag + y_off + dfull_ref[:, h * 64:(h + 1) * 64] * xs_h)
        segs[...] = jnp.where(zrow <= cidx, segs[...] + acs_c[CHUNK - 1:CHUNK, :], 0.0)
        y = jnp.concatenate(ys, axis=-1) * zg[r0:r0 + CHUNK]
        halves = []
        for g in range(2):
            yg = y[:, g * 128:(g + 1) * 128]
            halves.append(yg * lax.rsqrt(jnp.mean(yg * yg, axis=-1, keepdims=True) + EPS))
        y_ref[0, r0:r0 + CHUNK, :] = jnp.concatenate(halves, axis=-1) * nw_ref[...]

    @pl.when(j == pl.num_programs(1) - 1)
    def _():
        h_fin = entering_state((j + 1) * (tl // CHUNK) + 1)
        for h in range(SSD_HEADS):
            hout_ref[0, h] = h_fin[h]


def _ssd(z, xbc, dt, hist8, h0, cw, cb, dtb, alog, dfull, nw, tl):
    b, l, _ = z.shape
    n_states = l // CHUNK + 1
    n_states_pad = -(-n_states // 8) * 8
    row = lambda width: pl.BlockSpec((1, tl, width), lambda i, j: (i, j, 0))
    const = lambda shape: pl.BlockSpec(shape, lambda i, j: (0,) * len(shape))
    return pl.pallas_call(
        functools.partial(_ssd_kernel, tl=tl),
        grid=(b, l // tl),
        in_specs=[row(256), row(512), row(LANES),
                  pl.BlockSpec((1, 8, SSD_CONV_CH), lambda i, j: (i, 0, 0)),
                  pl.BlockSpec((1, SSD_HEADS, SSD_HEAD_DIM, SSD_STATE), lambda i, j: (i, 0, 0, 0)),
                  const((SSD_CONV, SSD_CONV_CH)), const((1, SSD_CONV_CH)), const((1, LANES)),
                  const((1, LANES)), const((1, MIX)), const((1, MIX))],
        out_specs=[row(256),
                   pl.BlockSpec((1, SSD_HEADS, SSD_HEAD_DIM, SSD_STATE), lambda i, j: (i, 0, 0, 0))],
        out_shape=[jax.ShapeDtypeStruct((b, l, MIX), F32),
                   jax.ShapeDtypeStruct((b, SSD_HEADS, SSD_HEAD_DIM, SSD_STATE), F32)],
        scratch_shapes=[pltpu.VMEM((8 + tl, SSD_CONV_CH), F32),
                        pltpu.VMEM((n_states, SSD_HEADS, SSD_HEAD_DIM, SSD_STATE), F32),
                        pltpu.VMEM((n_states_pad, LANES), F32),
                        pltpu.VMEM((n_states_pad, LANES), F32)],
        compiler_params=_cparams(2),
        name="ssd",
    )(z, xbc, dt, hist8, h0, cw, cb, dtb, alog, dfull, nw)


CCM_PAD = 32
CCM_SUB = 64


def _ccm_kernel(u_ref, hist_ref, w_ref, b_ref, lng_ref, lnb_ref, pw_ref, pwb_ref, y_ref, tail_ref, vpad, *, tl):
    j = pl.program_id(1)

    @pl.when(j == 0)
    def _():
        vpad[0:CCM_PAD, :] = _round_bf16(hist_ref[0])

    @pl.when(j > 0)
    def _():
        vpad[0:CCM_PAD, :] = vpad[tl:tl + CCM_PAD, :]

    u = u_ref[0]
    glu = u[:, :MIX] * _sigmoid(u[:, MIX:])
    vpad[CCM_PAD:CCM_PAD + tl, :] = _round_bf16(glu)
    taps = _round_bf16(w_ref[...])
    off = CCM_PAD - (CCM_WIDTH - 1)
    for s in range(tl // CCM_SUB):
        r0 = s * CCM_SUB
        acc = taps[0:1, :] * vpad[r0 + off:r0 + off + CCM_SUB, :]
        for kk in range(1, CCM_WIDTH):
            acc = acc + taps[kk:kk + 1, :] * vpad[r0 + off + kk:r0 + off + kk + CCM_SUB, :]
        acc = acc + b_ref[...]
        mu = jnp.mean(acc, axis=-1, keepdims=True)
        cen = acc - mu
        var = jnp.mean(cen * cen, axis=-1, keepdims=True)
        v = _silu(cen * lax.rsqrt(var + EPS) * lng_ref[...] + lnb_ref[...])
        y_ref[0, r0:r0 + CCM_SUB, :] = _dot(v.astype(BF16), pw_ref[...]) + pwb_ref[...]

    @pl.when(j == pl.num_programs(1) - 1)
    def _():
        tail_ref[0] = glu[tl - CCM_PAD:tl, :]


def _ccm(u, hist32, w32, bias, lng, lnb, pw, pwb, tl):
    b, l, _ = u.shape
    assert tl >= CCM_PAD
    const = lambda shape: pl.BlockSpec(shape, lambda i, j: (0,) * len(shape))
    return pl.pallas_call(
        functools.partial(_ccm_kernel, tl=tl),
        grid=(b, l // tl),
        in_specs=[pl.BlockSpec((1, tl, 2 * MIX), lambda i, j: (i, j, 0)),
                  pl.BlockSpec((1, CCM_PAD, MIX), lambda i, j: (i, 0, 0)),
                  const((CCM_PAD, MIX)), const((1, MIX)), const((1, MIX)), const((1, MIX)),
                  const((MIX, MIX)), const((1, MIX))],
        out_specs=[pl.BlockSpec((1, tl, MIX), lambda i, j: (i, j, 0)),
                   pl.BlockSpec((1, CCM_PAD, MIX), lambda i, j: (i, 0, 0))],
        out_shape=[jax.ShapeDtypeStruct((b, l, MIX), F32),
                   jax.ShapeDtypeStruct((b, CCM_PAD, MIX), F32)],
        scratch_shapes=[pltpu.VMEM((CCM_PAD + tl, MIX), F32)],
        compiler_params=_cparams(2),
        name="ccm",
    )(u, hist32, w32, bias, lng, lnb, pw, pwb)


def _attn_kernel(*refs, seq, qb, has_prefix):
    if has_prefix:
        q_ref, k_ref, v_ref, pk_ref, pv_ref, bias_ref, o_ref, kpad, vpad = refs
    else:
        q_ref, k_ref, v_ref, bias_ref, o_ref, kpad, vpad = refs
    c = pl.program_id(1)

    @pl.when(c == 0)
    def _():
        if has_prefix:
            kpad[0:ATT_WINDOW, :] = pk_ref[0].astype(BF16)
            vpad[0:ATT_WINDOW, :] = pv_ref[0].astype(BF16)
        else:
            kpad[0:ATT_WINDOW, :] = jnp.zeros((ATT_WINDOW, MIX), BF16)
            vpad[0:ATT_WINDOW, :] = jnp.zeros((ATT_WINDOW, MIX), BF16)
        kpad[ATT_WINDOW:ATT_WINDOW + seq, :] = k_ref[0].astype(BF16)
        vpad[ATT_WINDOW:ATT_WINDOW + seq, :] = v_ref[0].astype(BF16)

    win = ATT_WINDOW + qb
    start = pl.multiple_of(c * qb, qb)
    kwin = kpad[pl.ds(start, win), :]
    vwin = vpad[pl.ds(start, win), :]
    q = q_ref[0].astype(BF16)
    if not has_prefix:
        col = lax.broadcasted_iota(jnp.int32, (qb, win), 1)
        valid = col >= (ATT_WINDOW - c * qb)
    outs = []
    for h in range(ATT_HEADS):
        hs = slice(h * ATT_HEAD_DIM, (h + 1) * ATT_HEAD_DIM)
        s = _dot_nt(q[:, hs], kwin[:, hs]) + bias_ref[h]
        if not has_prefix:
            s = jnp.where(valid, s, -jnp.inf)
        m = jnp.max(s, axis=-1, keepdims=True)
        e = jnp.exp(s - m)
        p = e * (1.0 / jnp.sum(e, axis=-1, keepdims=True))
        outs.append(_dot(p.astype(BF16), vwin[:, hs]))
    o_ref[0] = jnp.concatenate(outs, axis=-1)


def _attn_bias(table, qb):
    r = jnp.arange(qb)[:, None]
    j = jnp.arange(ATT_WINDOW + qb)[None, :]
    rel = j - (r // CHUNK) * CHUNK
    bias = table.astype(F32)[:, jnp.clip(ATT_WINDOW + r - j, -REL_MAX, REL_MAX) + REL_MAX]
    return jnp.where(jnp.logical_and(rel >= 0, rel < ATT_BAND)[None], bias, -jnp.inf)


def _attn(q, k, v, table, prefix=None):
    b, l, _ = q.shape
    qb = _tile(l, 256)
    has_prefix = prefix is not None
    full = pl.BlockSpec((1, l, MIX), lambda i, j: (i, 0, 0))
    in_specs = [pl.BlockSpec((1, qb, MIX), lambda i, j: (i, j, 0)), full, full]
    args = [q, k, v]
    if has_prefix:
        pre = pl.BlockSpec((1, ATT_WINDOW, MIX), lambda i, j: (i, 0, 0))
        in_specs += [pre, pre]
        args += list(prefix)
    in_specs.append(pl.BlockSpec((ATT_HEADS, qb, ATT_WINDOW + qb), lambda i, j: (0, 0, 0)))
    args.append(_attn_bias(table, qb))
    return pl.pallas_call(
        functools.partial(_attn_kernel, seq=l, qb=qb, has_prefix=has_prefix),
        grid=(b, l // qb),
        in_specs=in_specs,
        out_specs=pl.BlockSpec((1, qb, MIX), lambda i, j: (i, j, 0)),
        out_shape=jax.ShapeDtypeStruct((b, l, MIX), F32),
        scratch_shapes=[pltpu.VMEM((ATT_WINDOW + l, MIX), BF16),
                        pltpu.VMEM((ATT_WINDOW + l, MIX), BF16)],
        compiler_params=_cparams(2),
        name="attn",
    )(*args)


def _gelu_tanh(x):
    return 0.5 * x * (1.0 + jnp.tanh(math.sqrt(2.0 / math.pi) * (x + 0.044715 * (x * x * x))))


def _s5_kernel(u_ref, h0_ref, bb_ref, cc_ref, a_ref, d_ref, gw_ref, gb_ref, y_ref, hout_ref, xs, hst, *, tc):
    j = pl.program_id(1)

    @pl.when(j == 0)
    def _():
        hst[...] = h0_ref[0]

    u = u_ref[0]
    xs[...] = _dot(u.astype(BF16), bb_ref[...])
    ar = a_ref[:, :S5_MODES]
    ai = a_ref[:, S5_MODES:]

    def step(t, carry):
        hr, hi = carry
        r0 = pl.multiple_of(t * S5_STREAMS, S5_STREAMS)
        xr = xs[pl.ds(r0, S5_STREAMS), 0:S5_MODES]
        xi = xs[pl.ds(r0, S5_STREAMS), S5_MODES:2 * S5_MODES]
        nr = ar * hr - ai * hi + xr
        ni = ar * hi + ai * hr + xi
        xs[pl.ds(r0, S5_STREAMS), 0:S5_MODES] = nr
        xs[pl.ds(r0, S5_STREAMS), S5_MODES:2 * S5_MODES] = ni
        return nr, ni

    hr, hi = lax.fori_loop(0, tc, step, (hst[:, :S5_MODES], hst[:, S5_MODES:]))
    hst[:, :S5_MODES] = hr
    hst[:, S5_MODES:] = hi

    y = _dot(xs[...].astype(BF16), cc_ref[...]) + d_ref[...] * u
    gl = _dot(_gelu_tanh(y).astype(BF16), gw_ref[...]) + gb_ref[...]
    y_ref[0] = gl[:, :MIX] * _sigmoid(gl[:, MIX:])

    @pl.when(j == pl.num_programs(1) - 1)
    def _():
        hout_ref[0] = hst[...]


def _s5(u5, h0, bb, cc, a8, dvec, gw, gb, tc):
    nb, rows, _ = u5.shape
    blk = tc * S5_STREAMS
    const = lambda shape: pl.BlockSpec(shape, lambda i, j: (0,) * len(shape))
    return pl.pallas_call(
        functools.partial(_s5_kernel, tc=tc),
        grid=(nb, rows // blk),
        in_specs=[pl.BlockSpec((1, blk, MIX), lambda i, j: (i, j, 0)),
                  pl.BlockSpec((1, S5_STREAMS, 2 * S5_MODES), lambda i, j: (i, 0, 0)),
                  const((MIX, 2 * S5_MODES)), const((2 * S5_MODES, MIX)), const((S5_STREAMS, 2 * S5_MODES)),
                  const((1, MIX)), const((MIX, 2 * MIX)), const((1, 2 * MIX))],
        out_specs=[pl.BlockSpec((1, blk, MIX), lambda i, j: (i, j, 0)),
                   pl.BlockSpec((1, S5_STREAMS, 2 * S5_MODES), lambda i, j: (i, 0, 0))],
        out_shape=[jax.ShapeDtypeStruct((nb, rows, MIX), F32),
                   jax.ShapeDtypeStruct((nb, S5_STREAMS, 2 * S5_MODES), F32)],
        scratch_shapes=[pltpu.VMEM((blk, 2 * S5_MODES), F32),
                        pltpu.VMEM((S5_STREAMS, 2 * S5_MODES), F32)],
        compiler_params=_cparams(2),
        name="s5",
    )(u5, h0, bb, cc, a8, dvec, gw, gb)


def _out_router_kernel(x_ref, ya_ref, yb_ref, yc_ref, yd_ref, wout_ref, g_ref, wr_ref, tril_ref,
                       h_ref, hn_ref, route_ref, cnt_ref, base, *, tm):
    first = jnp.logical_and(pl.program_id(0) == 0, pl.program_id(1) == 0)

    @pl.when(first)
    def _():
        base[...] = jnp.zeros_like(base)

    mix = jnp.concatenate([ya_ref[0], yb_ref[0], yc_ref[0], yd_ref[0]], axis=-1).astype(BF16)
    h = x_ref[0] + _dot(mix, wout_ref[...])
    h_ref[0] = h
    hb = _rms(h, g_ref[...]).astype(BF16)
    hn_ref[0] = _pack_bf16_pairs(hb)
    lg = _dot(hb, wr_ref[...])

    lane = lax.broadcasted_iota(jnp.int32, (tm, LANES), 1).astype(F32)
    big = float(LANES)
    is_grp = lane < MOE_GROUPS
    lgg = jnp.where(is_grp, lg, -jnp.inf)
    mg = jnp.max(lgg, axis=-1, keepdims=True)
    grp = jnp.min(jnp.where(lgg == mg, lane, big), axis=-1, keepdims=True)
    p_grp = 1.0 / jnp.sum(jnp.where(is_grp, jnp.exp(lgg - mg), 0.0), axis=-1, keepdims=True)
    lo = ROUTE_LANE0 + MOE_EPG * grp
    in_grp = jnp.logical_and(lane >= lo, lane < lo + MOE_EPG)
    le = jnp.where(in_grp, lg, -jnp.inf)
    t1 = jnp.max(le, axis=-1, keepdims=True)
    i1 = jnp.min(jnp.where(le == t1, lane, big), axis=-1, keepdims=True)
    le2 = jnp.where(lane == i1, -jnp.inf, le)
    t2 = jnp.max(le2, axis=-1, keepdims=True)
    i2 = jnp.min(jnp.where(le2 == t2, lane, big), axis=-1, keepdims=True)
    e21 = jnp.exp(t2 - t1)
    inv = 1.0 / (1.0 + e21)
    g1 = inv * p_grp
    g2 = (e21 * inv) * p_grp

    oh1 = lane == i1
    oh2 = lane == i2
    ohs = jnp.where(jnp.logical_or(oh1, oh2), 1.0, 0.0)
    cnt = _dot(tril_ref[...], ohs.astype(BF16)) + base[...]
    r1 = jnp.sum(jnp.where(oh1, cnt, 0.0), axis=-1, keepdims=True)
    r2 = jnp.sum(jnp.where(oh2, cnt, 0.0), axis=-1, keepdims=True)
    base[...] = base[...] + jnp.sum(ohs, axis=0, keepdims=True)

    route = jnp.where(lane == 0.0, i1 - ROUTE_LANE0, 0.0)
    route = jnp.where(lane == 1.0, i2 - ROUTE_LANE0, route)
    route = jnp.where(lane == 2.0, r1, route)
    route = jnp.where(lane == 3.0, r2, route)
    route = jnp.where(lane == 4.0, g1, route)
    route = jnp.where(lane == 5.0, g2, route)
    route_ref[0] = route
    cnt_ref[...] = jnp.broadcast_to(base[...], (8, LANES))


def _out_router(x, ya, yb, yc, yd5, wout, g, wr, tril, tm):
    b, l, _ = x.shape
    row = lambda width: pl.BlockSpec((1, tm, width), lambda i, j: (i, j, 0))
    const = lambda shape: pl.BlockSpec(shape, lambda i, j: (0,) * len(shape))
    return pl.pallas_call(
        functools.partial(_out_router_kernel, tm=tm),
        grid=(b, l // tm),
        in_specs=[row(D_MODEL), row(MIX), row(MIX), row(MIX),
                  pl.BlockSpec((1, tm, MIX), lambda i, j: (i // S5_STREAMS, j, i % S5_STREAMS)),
                  const((D_MODEL, D_MODEL)), const((1, D_MODEL)), const((D_MODEL, LANES)), const((tm, tm))],
        out_specs=[row(D_MODEL), row(HALF), row(LANES), const((8, LANES))],
        out_shape=[jax.ShapeDtypeStruct((b, l, D_MODEL), F32),
                   jax.ShapeDtypeStruct((b, l, HALF), jnp.uint32),
                   jax.ShapeDtypeStruct((b, l, LANES), F32),
                   jax.ShapeDtypeStruct((8, LANES), F32)],
        scratch_shapes=[pltpu.VMEM((1, LANES), F32)],
        compiler_params=_cparams(2),
        name="out_router",
    )(x, ya, yb, yc, yd5, wout, g, wr, tril)


HALF = D_MODEL // 2
HI_MASK = 0xFFFF0000


def _pack_bf16_pairs(xb):
    bits = lax.bitcast_convert_type(xb.astype(F32), jnp.uint32)
    return (bits[:, :HALF] >> 16) | (bits[:, HALF:] & jnp.uint32(HI_MASK))


def _unpack_bf16_pairs(w):
    lo = lax.bitcast_convert_type(w << 16, F32)
    hi = lax.bitcast_convert_type(w & jnp.uint32(HI_MASK), F32)
    return lo.astype(BF16), hi.astype(BF16)


def _row_copy(src, src_row, dst, dst_row, sem):
    return pltpu.make_async_copy(src.at[pl.ds(src_row, 1), :], dst.at[pl.ds(dst_row, 1), :], sem)


def _dispatch_kernel(dest_ref, hn_ref, xs_in_ref, xs_ref, sem, *, tm):
    del xs_in_ref

    def issue(blk, carry):
        for u in range(DMA_UNROLL):
            i = blk * DMA_UNROLL + u
            for kk in range(2):
                _row_copy(hn_ref, i, xs_ref, dest_ref[0, 0, 2 * i + kk], sem).start()
        return carry

    lax.fori_loop(0, tm // DMA_UNROLL, issue, 0)
    def drain(blk, carry):
        for _ in range(2 * DMA_UNROLL):
            _row_copy(hn_ref, 0, xs_ref, 0, sem).wait()
        return carry

    lax.fori_loop(0, tm // DMA_UNROLL, drain, 0)


def _dispatch(dest3, hn, xs0, tm):
    t = hn.shape[0]
    return pl.pallas_call(
        functools.partial(_dispatch_kernel, tm=tm),
        grid=(t // tm,),
        in_specs=[pl.BlockSpec((1, 1, 2 * tm), lambda i: (i, 0, 0), memory_space=pltpu.SMEM),
                  pl.BlockSpec((tm, HALF), lambda i: (i, 0)),
                  pl.BlockSpec(memory_space=pl.ANY)],
        out_specs=pl.BlockSpec(memory_space=pl.ANY),
        out_shape=jax.ShapeDtypeStruct(xs0.shape, xs0.dtype),
        scratch_shapes=[pltpu.SemaphoreType.DMA(())],
        input_output_aliases={2: 0},
        compiler_params=_cparams(1),
        name="dispatch",
    )(dest3, hn, xs0)


def _ffn_kernel(be_ref, xs_ref, wg_ref, wu_ref, wd_ref, ys_ref):
    del be_ref
    lo, hi = _unpack_bf16_pairs(xs_ref[...])
    a = _dot(lo, wg_ref[0, :HALF, :]) + _dot(hi, wg_ref[0, HALF:, :])
    v = _dot(lo, wu_ref[0, :HALF, :]) + _dot(hi, wu_ref[0, HALF:, :])
    ys_ref[...] = _dot((_silu(a) * v).astype(BF16), wd_ref[0])


def _ffn(blk_expert, xs, wg, wu, wd):
    n_rows = xs.shape[0]
    grid_spec = pltpu.PrefetchScalarGridSpec(
        num_scalar_prefetch=1,
        grid=(n_rows // MOE_ROWS,),
        in_specs=[pl.BlockSpec((MOE_ROWS, HALF), lambda i, be: (i, 0)),
                  pl.BlockSpec((1, D_MODEL, MOE_D_FF), lambda i, be: (be[i], 0, 0)),
                  pl.BlockSpec((1, D_MODEL, MOE_D_FF), lambda i, be: (be[i], 0, 0)),
                  pl.BlockSpec((1, MOE_D_FF, D_MODEL), lambda i, be: (be[i], 0, 0))],
        out_specs=pl.BlockSpec((MOE_ROWS, D_MODEL), lambda i, be: (i, 0)),
    )
    return pl.pallas_call(
        _ffn_kernel,
        grid_spec=grid_spec,
        out_shape=jax.ShapeDtypeStruct((n_rows, D_MODEL), F32),
        compiler_params=_cparams(1),
        name="expert_ffn",
    )(blk_expert, xs, wg, wu, wd)


def _combine_kernel(dest_ref, h_ref, route_ref, gfin_ref, ys_ref, out_ref, buf0, buf1, sem, *, tm, final):
    def issue(blk, carry):
        for u in range(DMA_UNROLL):
            i = blk * DMA_UNROLL + u
            _row_copy(ys_ref, dest_ref[0, 0, 2 * i], buf0, i, sem).start()
            _row_copy(ys_ref, dest_ref[0, 0, 2 * i + 1], buf1, i, sem).start()
        return carry

    lax.fori_loop(0, tm // DMA_UNROLL, issue, 0)
    def drain(blk, carry):
        for _ in range(2 * DMA_UNROLL):
            _row_copy(ys_ref, 0, buf0, 0, sem).wait()
        return carry

    lax.fori_loop(0, tm // DMA_UNROLL, drain, 0)
    route = route_ref[...]
    h = h_ref[...] + (route[:, 4:5] * buf0[...] + route[:, 5:6] * buf1[...])
    out_ref[...] = _rms(h, gfin_ref[...]) if final else h


def _combine(dest3, h, route, gfin, ys, tm, final):
    t = h.shape[0]
    return pl.pallas_call(
        functools.partial(_combine_kernel, tm=tm, final=final),
        grid=(t // tm,),
        in_specs=[pl.BlockSpec((1, 1, 2 * tm), lambda i: (i, 0, 0), memory_space=pltpu.SMEM),
                  pl.BlockSpec((tm, D_MODEL), lambda i: (i, 0)),
                  pl.BlockSpec((tm, LANES), lambda i: (i, 0)),
                  pl.BlockSpec((1, D_MODEL), lambda i: (0, 0)),
                  pl.BlockSpec(memory_space=pl.ANY)],
        out_specs=pl.BlockSpec((tm, D_MODEL), lambda i: (i, 0)),
        out_shape=jax.ShapeDtypeStruct((t, D_MODEL), F32),
        scratch_shapes=[pltpu.VMEM((tm, D_MODEL), F32), pltpu.VMEM((tm, D_MODEL), F32),
                        pltpu.SemaphoreType.DMA(())],
        compiler_params=_cparams(1),
        name="combine",
    )(dest3, h, route, gfin, ys)


def _prep_layer(p):
    w_in = p["w_in"]
    z, xbc, dt, ccm, q, k, v, s5 = jnp.split(
        w_in, [256, 768, 772, 1284, 1540, 1796, 2052], axis=1)
    dt = jnp.pad(dt, ((0, 0), (0, LANES - SSD_HEADS)))
    w_in_p = jnp.concatenate([z, xbc, ccm, q, k, v, s5, dt], axis=1).astype(BF16)

    pad_l = lambda a: jnp.pad(a, (0, LANES - a.shape[0]))[None, :]

    lr, li = p["s5_lam_re"], p["s5_lam_im"]
    dts = jnp.exp(p["s5_log_dt"])[:, None]
    mag = jnp.exp(lr * dts)
    ar, ai = mag * jnp.cos(li * dts), mag * jnp.sin(li * dts)
    den = lr * lr + li * li
    cr = ((ar - 1.0) * lr + ai * li) / den
    ci = (ai * lr - (ar - 1.0) * li) / den
    br, bi = p["s5_b_re"], p["s5_b_im"]
    bbr = cr[..., None] * br - ci[..., None] * bi
    bbi = cr[..., None] * bi + ci[..., None] * br
    eye = jnp.eye(S5_GROUPS, dtype=F32)
    drive = lambda m: jnp.einsum("gnc,gk->gckn", m, eye).reshape(MIX, S5_MODES)
    bb = jnp.concatenate([drive(bbr), drive(bbi)], axis=1).astype(BF16)
    read = lambda m: jnp.einsum("gcn,gk->gnkc", m, eye).reshape(S5_MODES, MIX)
    cc = jnp.concatenate([read(p["s5_c_re"]), -read(p["s5_c_im"])], axis=0).astype(BF16)
    a8 = jnp.broadcast_to(jnp.concatenate([ar.reshape(-1), ai.reshape(-1)])[None, :],
                          (S5_STREAMS, 2 * S5_MODES))

    w_route = jnp.concatenate(
        [p["moe_w_group"], jnp.transpose(p["moe_w_expert"], (1, 0, 2)).reshape(D_MODEL, MOE_EXPERTS)], axis=1)
    w_route = jnp.pad(w_route, ((0, 0), (0, LANES - w_route.shape[1]))).astype(BF16)
    return dict(
        g_mix=p["g_mix"][None, :], w_in=w_in_p,
        ssd_cw=p["ssd_conv_w"], ssd_cb=p["ssd_conv_b"][None, :], ssd_dtb=pad_l(p["ssd_dt_bias"]),
        ssd_alog=pad_l(p["ssd_a_log"]), ssd_dfull=jnp.repeat(p["ssd_d"], SSD_HEAD_DIM)[None, :],
        ssd_nw=p["ssd_norm_w"][None, :],
        ccm_w=jnp.pad(p["ccm_dw_w"], ((0, CCM_PAD - CCM_WIDTH), (0, 0))), ccm_b=p["ccm_dw_b"][None, :],
        ccm_lng=p["ccm_ln_g"][None, :], ccm_lnb=p["ccm_ln_b"][None, :],
        ccm_pw=p["ccm_pw_w"].astype(BF16), ccm_pwb=p["ccm_pw_b"][None, :],
        att_table=p["att_rel_bias"],
        s5_bb=bb, s5_cc=cc, s5_a8=a8, s5_d=p["s5_d"].reshape(1, MIX),
        s5_gw=p["s5_glu_w"].astype(BF16), s5_gb=p["s5_glu_b"][None, :],
        w_out=p["w_out"].astype(BF16), g_ffn=p["g_ffn"][None, :], w_route=w_route,
        moe_wg=p["moe_w_gate"].astype(BF16), moe_wu=p["moe_w_up"].astype(BF16),
        moe_wd=p["moe_w_down"].astype(BF16),
    )


def _tile(n, pref):
    return pref if n % pref == 0 else n


def _layer(x, w, st, g_final, final):
    b, l, _ = x.shape
    t = b * l
    nb = b // S5_STREAMS
    tm = _tile(l, 512)
    z, xbc, dt, ccm, q, k, v, u5 = _in_proj(x, w["g_mix"], w["w_in"], tm)

    tl = _tile(l, 256)
    hist8 = jnp.pad(st["ssd_conv"], ((0, 0), (8 - (SSD_CONV - 1), 0), (0, 0)))
    y_a, ssd_new = _ssd(z, xbc, dt, hist8, st["ssd"], w["ssd_cw"], w["ssd_cb"], w["ssd_dtb"],
                        w["ssd_alog"], w["ssd_dfull"], w["ssd_nw"], tl)
    ssd_conv_new = jnp.concatenate([st["ssd_conv"], xbc], axis=1)[:, -(SSD_CONV - 1):]

    hist32 = jnp.pad(st["ccm_conv"], ((0, 0), (CCM_PAD - (CCM_WIDTH - 1), 0), (0, 0)))
    y_b, ccm_tail = _ccm(ccm, hist32, w["ccm_w"], w["ccm_b"], w["ccm_lng"], w["ccm_lnb"],
                         w["ccm_pw"], w["ccm_pwb"], tl)
    ccm_conv_new = ccm_tail[:, CCM_PAD - (CCM_WIDTH - 1):]

    if "att_k" in st:
        prefix = (st["att_k"].reshape(b, ATT_WINDOW, MIX), st["att_v"].reshape(b, ATT_WINDOW, MIX))
        y_c = _attn(q, k, v, w["att_table"], prefix)
    else:
        y_c = _attn(q, k, v, w["att_table"])
    keep = min(ATT_WINDOW, l)
    k_new = k[:, l - keep:].reshape(b, keep, ATT_HEADS, ATT_HEAD_DIM)
    v_new = v[:, l - keep:].reshape(b, keep, ATT_HEADS, ATT_HEAD_DIM)

    h0 = jnp.concatenate([st["s5_re"].reshape(nb, S5_STREAMS, S5_MODES),
                          st["s5_im"].reshape(nb, S5_STREAMS, S5_MODES)], axis=-1)
    y_d5, s5_new = _s5(u5.reshape(nb, l * S5_STREAMS, MIX), h0, w["s5_bb"], w["s5_cc"], w["s5_a8"],
                       w["s5_d"], w["s5_gw"], w["s5_gb"], _tile(l, 64))
    s5_re_new = s5_new[:, :, :S5_MODES].reshape(b, S5_GROUPS, S5_STATE)
    s5_im_new = s5_new[:, :, S5_MODES:].reshape(b, S5_GROUPS, S5_STATE)

    tril = jnp.tril(jnp.ones((tm, tm), BF16), -1)
    h, hn, route, cnt = _out_router(x, y_a, y_b, y_c, y_d5.reshape(nb, l, S5_STREAMS * MIX),
                                    w["w_out"], w["g_ffn"], w["w_route"], tril, tm)

    route = route.reshape(t, LANES)
    counts = cnt[0, ROUTE_LANE0:ROUTE_LANE0 + MOE_EXPERTS].astype(jnp.int32)
    padded = (counts + MOE_ROWS - 1) // MOE_ROWS * MOE_ROWS
    pad_end = jnp.cumsum(padded)
    pad_start = pad_end - padded
    n_blocks = -(-(2 * t) // MOE_ROWS) + MOE_EXPERTS
    experts = jnp.arange(MOE_EXPERTS, dtype=jnp.int32)
    is_e = route[:, 0:2].astype(jnp.int32)[..., None] == experts
    dest = jnp.sum(jnp.where(is_e, pad_start, 0), axis=-1) + route[:, 2:4].astype(jnp.int32)
    blk_row0 = jnp.arange(n_blocks, dtype=jnp.int32) * MOE_ROWS
    blk_expert = jnp.minimum(jnp.sum((pad_end[None, :] <= blk_row0[:, None]).astype(jnp.int32), axis=1),
                             MOE_EXPERTS - 1)

    td = _tile(t, 256)
    dest3 = dest.reshape(t // td, 1, 2 * td)
    xs = _dispatch(dest3, hn.reshape(t, HALF), jnp.zeros((n_blocks * MOE_ROWS, HALF), jnp.uint32), td)
    ys = _ffn(blk_expert, xs, w["moe_wg"], w["moe_wu"], w["moe_wd"])
    out = _combine(dest3, h.reshape(t, D_MODEL), route, g_final, ys, td, final).reshape(b, l, D_MODEL)

    new = dict(ssd=ssd_new, ssd_conv=ssd_conv_new, ccm_conv=ccm_conv_new, att_k=k_new, att_v=v_new,
               s5_re=s5_re_new, s5_im=s5_im_new)
    return out, new


NAMES = ("ssd", "ssd_conv", "ccm_conv", "att_k", "att_v", "s5_re", "s5_im")


def kernel(x_prompt, x_sample, state_ssd, state_ssd_conv, state_ccm_conv, cache_att_k, cache_att_v, state_s5_re, state_s5_im, g_mix, w_in, ssd_conv_w, ssd_conv_b, ssd_dt_bias, ssd_a_log, ssd_d, ssd_norm_w, ccm_dw_w, ccm_dw_b, ccm_ln_g, ccm_ln_b, ccm_pw_w, ccm_pw_b, att_rel_bias, s5_lam_re, s5_lam_im, s5_log_dt, s5_b_re, s5_b_im, s5_c_re, s5_c_im, s5_d, s5_glu_w, s5_glu_b, w_out, g_ffn, moe_w_group, moe_w_expert, moe_w_gate, moe_w_up, moe_w_down, g_final):
    params = dict(g_mix=g_mix, w_in=w_in, ssd_conv_w=ssd_conv_w, ssd_conv_b=ssd_conv_b,
                  ssd_dt_bias=ssd_dt_bias, ssd_a_log=ssd_a_log, ssd_d=ssd_d, ssd_norm_w=ssd_norm_w,
                  ccm_dw_w=ccm_dw_w, ccm_dw_b=ccm_dw_b, ccm_ln_g=ccm_ln_g, ccm_ln_b=ccm_ln_b,
                  ccm_pw_w=ccm_pw_w, ccm_pw_b=ccm_pw_b, att_rel_bias=att_rel_bias, s5_lam_re=s5_lam_re,
                  s5_lam_im=s5_lam_im, s5_log_dt=s5_log_dt, s5_b_re=s5_b_re, s5_b_im=s5_b_im,
                  s5_c_re=s5_c_re, s5_c_im=s5_c_im, s5_d=s5_d, s5_glu_w=s5_glu_w, s5_glu_b=s5_glu_b,
                  w_out=w_out, g_ffn=g_ffn, moe_w_group=moe_w_group, moe_w_expert=moe_w_expert,
                  moe_w_gate=moe_w_gate, moe_w_up=moe_w_up, moe_w_down=moe_w_down)
    depth = w_in.shape[0]
    bp = x_prompt.shape[0]
    g_fin = g_final[None, :]
    new_p = {n: [] for n in NAMES}
    new_s = {n: [] for n in NAMES}
    yp, ys = x_prompt, x_sample
    for i in range(depth):
        w = _prep_layer({n: a[i] for n, a in params.items()})
        final = i == depth - 1
        st_p = dict(ssd=jnp.zeros((bp, SSD_HEADS, SSD_HEAD_DIM, SSD_STATE), F32),
                    ssd_conv=jnp.zeros((bp, SSD_CONV - 1, SSD_CONV_CH), F32),
                    ccm_conv=jnp.zeros((bp, CCM_WIDTH - 1, MIX), F32),
                    s5_re=jnp.zeros((bp, S5_GROUPS, S5_STATE), F32),
                    s5_im=jnp.zeros((bp, S5_GROUPS, S5_STATE), F32))
        yp, sp = _layer(yp, w, st_p, g_fin, final)
        st_s = dict(ssd=state_ssd[i], ssd_conv=state_ssd_conv[i], ccm_conv=state_ccm_conv[i],
                    att_k=cache_att_k[i], att_v=cache_att_v[i], s5_re=state_s5_re[i], s5_im=state_s5_im[i])
        ys, ss = _layer(ys, w, st_s, g_fin, final)
        for n in NAMES:
            new_p[n].append(sp[n])
            new_s[n].append(ss[n])
    sp = {n: jnp.stack(new_p[n]) for n in NAMES}
    ss = {n: jnp.stack(new_s[n]) for n in NAMES}
    return (yp, ys, sp["ssd"], ss["ssd"], sp["ssd_conv"], ss["ssd_conv"], sp["ccm_conv"], ss["ccm_conv"],
            sp["att_k"], ss["att_k"], sp["att_v"], ss["att_v"], sp["s5_re"], ss["s5_re"],
            sp["s5_im"], ss["s5_im"])
```

```python
import functools
import math

import jax
import jax.numpy as jnp
from jax import lax
from jax.experimental import pallas as pl
from jax.experimental.pallas import tpu as pltpu

F32 = jnp.float32
BF16 = jnp.bfloat16

D_MODEL = 1024
CHUNK = 64
EPS = 1e-6
MIX = 256
SSD_HEADS = 4
SSD_HEAD_DIM = 64
SSD_STATE = 64
SSD_CONV = 4
SSD_CONV_CH = 512
CCM_WIDTH = 31
ATT_HEADS = 4
ATT_HEAD_DIM = 64
ATT_WINDOW = 512
ATT_BAND = ATT_WINDOW + CHUNK
ATT_SCALE = ATT_HEAD_DIM ** -0.5
REL_MAX = 128
S5_GROUPS = 16
S5_GROUP_CH = 16
S5_STATE = 64
S5_MODES = S5_GROUPS * S5_STATE
S5_STREAMS = 8
MOE_GROUPS = 4
MOE_EPG = 8
MOE_EXPERTS = 32
MOE_D_FF = 512
MOE_ROWS = 512
DMA_UNROLL = 8
LANES = 128
ROUTE_LANE0 = MOE_GROUPS

COL_Z, COL_XBC, COL_CCM, COL_Q, COL_K, COL_V, COL_S5, COL_DT, COL_END = (
    0, 256, 768, 1280, 1536, 1792, 2048, 2304, 2432)

VMEM_LIMIT = 56 * 1024 * 1024


def _cparams(n_axes, vmem=VMEM_LIMIT):
    return pltpu.CompilerParams(dimension_semantics=("arbitrary",) * n_axes, vmem_limit_bytes=vmem)


def _dot(a, b):
    return jnp.dot(a, b, preferred_element_type=F32)


def _dot_nt(a, b):
    return lax.dot_general(a, b, (((1,), (1,)), ((), ())), preferred_element_type=F32)


def _dot_tn(a, b):
    return lax.dot_general(a, b, (((0,), (0,)), ((), ())), preferred_element_type=F32)


def _sigmoid(x):
    return 1.0 / (1.0 + jnp.exp(-x))


def _silu(x):
    return x * _sigmoid(x)


def _rms(x, g):
    return x * lax.rsqrt(jnp.mean(x * x, axis=-1, keepdims=True) + EPS) * g


def _in_proj_kernel(x_ref, g_ref, w_ref, z_ref, xbc_ref, dt_ref, ccm_ref, q_ref, k_ref, v_ref, u5_ref):
    xn = _rms(x_ref[0], g_ref[...]).astype(BF16)

    def mm(lo, hi):
        return _dot(xn, w_ref[:, lo:hi])

    z_ref[0] = mm(COL_Z, COL_XBC)
    xbc_ref[0] = mm(COL_XBC, COL_CCM)
    ccm_ref[0] = mm(COL_CCM, COL_Q)
    q_ref[0] = mm(COL_Q, COL_K) * ATT_SCALE
    k_ref[0] = mm(COL_K, COL_V)
    v_ref[0] = mm(COL_V, COL_S5)
    u5_ref[0] = mm(COL_S5, COL_DT)
    dt_ref[0] = mm(COL_DT, COL_END)


def _in_proj(x, g, w, tm):
    b, l, _ = x.shape
    nb = b // S5_STREAMS
    row = lambda width: pl.BlockSpec((1, tm, width), lambda i, j: (i, j, 0))
    outs = [
        jax.ShapeDtypeStruct((b, l, 256), F32),
        jax.ShapeDtypeStruct((b, l, 512), F32),
        jax.ShapeDtypeStruct((b, l, LANES), F32),
        jax.ShapeDtypeStruct((b, l, 512), F32),
        jax.ShapeDtypeStruct((b, l, 256), F32),
        jax.ShapeDtypeStruct((b, l, 256), F32),
        jax.ShapeDtypeStruct((b, l, 256), F32),
        jax.ShapeDtypeStruct((nb, l, S5_STREAMS * MIX), F32),
    ]
    out_specs = [row(256), row(512), row(LANES), row(512), row(256), row(256), row(256),
                 pl.BlockSpec((1, tm, MIX), lambda i, j: (i // S5_STREAMS, j, i % S5_STREAMS))]
    return pl.pallas_call(
        _in_proj_kernel,
        grid=(b, l // tm),
        in_specs=[row(D_MODEL),
                  pl.BlockSpec((1, D_MODEL), lambda i, j: (0, 0)),
                  pl.BlockSpec((D_MODEL, COL_END), lambda i, j: (0, 0))],
        out_specs=out_specs,
        out_shape=outs,
        compiler_params=_cparams(2),
        name="in_proj",
    )(x, g, w)


def _ssd_kernel(z_ref, xbc_ref, dt_ref, hist_ref, h0_ref, cw_ref, cb_ref, dtb_ref, alog_ref, dfull_ref,
                nw_ref, y_ref, hout_ref, xpad, hst, *, tl):
    j = pl.program_id(1)

    @pl.when(j == 0)
    def _():
        xpad[0:8, :] = hist_ref[0]
        hst[...] = h0_ref[0]

    @pl.when(j > 0)
    def _():
        xpad[0:8, :] = xpad[tl:tl + 8, :]

    xpad[8:8 + tl, :] = xbc_ref[0]
    conv = cb_ref[...]
    for kk in range(SSD_CONV):
        conv = conv + cw_ref[kk:kk + 1, :] * xpad[5 + kk:5 + kk + tl, :]
    xact = _silu(conv)

    raw = dt_ref[0] + dtb_ref[...]
    dt = jnp.maximum(raw, 0.0) + jnp.log(1.0 + jnp.exp(-jnp.abs(raw)))
    a_neg = -jnp.exp(alog_ref[...])
    acs = dt * a_neg
    row = lax.broadcasted_iota(jnp.int32, (tl, LANES), 0) & (CHUNK - 1)
    for s in (1, 2, 4, 8, 16, 32):
        acs = acs + jnp.where(row >= s, pltpu.roll(acs, s, axis=0), 0.0)
    acs_t = acs.T

    li = lax.broadcasted_iota(jnp.int32, (CHUNK, CHUNK), 0)
    si = lax.broadcasted_iota(jnp.int32, (CHUNK, CHUNK), 1)
    causal = li >= si
    zg = _silu(z_ref[0])
    for c in range(tl // CHUNK):
        r0 = c * CHUNK
        acs_c = acs[r0:r0 + CHUNK]
        acs_tc = acs_t[:, r0:r0 + CHUNK]
        dt_c = dt[r0:r0 + CHUNK]
        x_c = xact[r0:r0 + CHUNK]
        ys = []
        for g in range(2):
            bg = x_c[:, 256 + g * 64:256 + (g + 1) * 64].astype(BF16)
            cg = x_c[:, 384 + g * 64:384 + (g + 1) * 64].astype(BF16)
            cbm = _dot_nt(cg, bg)
            for h in (2 * g, 2 * g + 1):
                a_col = acs_c[:, h:h + 1]
                a_row = acs_tc[h:h + 1, :]
                lm = jnp.exp(jnp.where(causal, a_col - a_row, -jnp.inf))
                xs_h = x_c[:, h * 64:(h + 1) * 64]
                xd = xs_h * dt_c[:, h:h + 1]
                y_diag = _dot((cbm * lm).astype(BF16), xd.astype(BF16))
                a_last = acs_c[CHUNK - 1:CHUNK, h:h + 1]
                dec = jnp.exp(a_last - a_col)
                st = _dot_tn((xd * dec).astype(BF16), bg)
                h_prev = hst[h]
                y_off = jnp.exp(a_col) * _dot_nt(cg, h_prev.astype(BF16))
                hst[h] = jnp.exp(a_last) * h_prev + st
                ys.append(y_diag + y_off + dfull_ref[:, h * 64:(h + 1) * 64] * xs_h)
        y = jnp.concatenate(ys, axis=-1) * zg[r0:r0 + CHUNK]
        halves = []
        for g in range(2):
            yg = y[:, g * 128:(g + 1) * 128]
            halves.append(yg * lax.rsqrt(jnp.mean(yg * yg, axis=-1, keepdims=True) + EPS))
        y_ref[0, r0:r0 + CHUNK, :] = jnp.concatenate(halves, axis=-1) * nw_ref[...]

    @pl.when(j == pl.num_programs(1) - 1)
    def _():
        hout_ref[0] = hst[...]


def _ssd(z, xbc, dt, hist8, h0, cw, cb, dtb, alog, dfull, nw, tl):
    b, l, _ = z.shape
    row = lambda width: pl.BlockSpec((1, tl, width), lambda i, j: (i, j, 0))
    const = lambda shape: pl.BlockSpec(shape, lambda i, j: (0,) * len(shape))
    return pl.pallas_call(
        functools.partial(_ssd_kernel, tl=tl),
        grid=(b, l // tl),
        in_specs=[row(256), row(512), row(LANES),
                  pl.BlockSpec((1, 8, SSD_CONV_CH), lambda i, j: (i, 0, 0)),
                  pl.BlockSpec((1, SSD_HEADS, SSD_HEAD_DIM, SSD_STATE), lambda i, j: (i, 0, 0, 0)),
                  const((SSD_CONV, SSD_CONV_CH)), const((1, SSD_CONV_CH)), const((1, LANES)),
                  const((1, LANES)), const((1, MIX)), const((1, MIX))],
        out_specs=[row(256),
                   pl.BlockSpec((1, SSD_HEADS, SSD_HEAD_DIM, SSD_STATE), lambda i, j: (i, 0, 0, 0))],
        out_shape=[jax.ShapeDtypeStruct((b, l, MIX), F32),
                   jax.ShapeDtypeStruct((b, SSD_HEADS, SSD_HEAD_DIM, SSD_STATE), F32)],
        scratch_shapes=[pltpu.VMEM((8 + tl, SSD_CONV_CH), F32),
                        pltpu.VMEM((SSD_HEADS, SSD_HEAD_DIM, SSD_STATE), F32)],
        compiler_params=_cparams(2),
        name="ssd",
    )(z, xbc, dt, hist8, h0, cw, cb, dtb, alog, dfull, nw)


CCM_PAD = 32
CCM_SUB = 64


def _ccm_kernel(u_ref, hist_ref, w_ref, b_ref, lng_ref, lnb_ref, pw_ref, pwb_ref, y_ref, tail_ref, vpad, *, tl):
    j = pl.program_id(1)

    @pl.when(j == 0)
    def _():
        vpad[0:CCM_PAD, :] = hist_ref[0]

    @pl.when(j > 0)
    def _():
        vpad[0:CCM_PAD, :] = vpad[tl:tl + CCM_PAD, :]

    u = u_ref[0]
    vpad[CCM_PAD:CCM_PAD + tl, :] = u[:, :MIX] * _sigmoid(u[:, MIX:])
    off = CCM_PAD - (CCM_WIDTH - 1)
    for s in range(tl // CCM_SUB):
        r0 = s * CCM_SUB
        acc = jnp.broadcast_to(b_ref[...], (CCM_SUB, MIX))
        for kk in range(CCM_WIDTH):
            acc = acc + w_ref[kk:kk + 1, :] * vpad[r0 + off + kk:r0 + off + kk + CCM_SUB, :]
        mu = jnp.mean(acc, axis=-1, keepdims=True)
        cen = acc - mu
        var = jnp.mean(cen * cen, axis=-1, keepdims=True)
        v = _silu(cen * lax.rsqrt(var + EPS) * lng_ref[...] + lnb_ref[...])
        y_ref[0, r0:r0 + CCM_SUB, :] = _dot(v.astype(BF16), pw_ref[...]) + pwb_ref[...]

    @pl.when(j == pl.num_programs(1) - 1)
    def _():
        tail_ref[0] = vpad[tl:tl + CCM_PAD, :]


def _ccm(u, hist32, w32, bias, lng, lnb, pw, pwb, tl):
    b, l, _ = u.shape
    const = lambda shape: pl.BlockSpec(shape, lambda i, j: (0,) * len(shape))
    return pl.pallas_call(
        functools.partial(_ccm_kernel, tl=tl),
        grid=(b, l // tl),
        in_specs=[pl.BlockSpec((1, tl, 2 * MIX), lambda i, j: (i, j, 0)),
                  pl.BlockSpec((1, CCM_PAD, MIX), lambda i, j: (i, 0, 0)),
                  const((CCM_PAD, MIX)), const((1, MIX)), const((1, MIX)), const((1, MIX)),
                  const((MIX, MIX)), const((1, MIX))],
        out_specs=[pl.BlockSpec((1, tl, MIX), lambda i, j: (i, j, 0)),
                   pl.BlockSpec((1, CCM_PAD, MIX), lambda i, j: (i, 0, 0))],
        out_shape=[jax.ShapeDtypeStruct((b, l, MIX), F32),
                   jax.ShapeDtypeStruct((b, CCM_PAD, MIX), F32)],
        scratch_shapes=[pltpu.VMEM((CCM_PAD + tl, MIX), F32)],
        compiler_params=_cparams(2),
        name="ccm",
    )(u, hist32, w32, bias, lng, lnb, pw, pwb)


def _attn_kernel(*refs, seq, qb, has_prefix):
    if has_prefix:
        q_ref, k_ref, v_ref, pk_ref, pv_ref, bias_ref, o_ref, kpad, vpad = refs
    else:
        q_ref, k_ref, v_ref, bias_ref, o_ref, kpad, vpad = refs
    c = pl.program_id(1)

    @pl.when(c == 0)
    def _():
        if has_prefix:
            kpad[0:ATT_WINDOW, :] = pk_ref[0].astype(BF16)
            vpad[0:ATT_WINDOW, :] = pv_ref[0].astype(BF16)
        else:
            kpad[0:ATT_WINDOW, :] = jnp.zeros((ATT_WINDOW, MIX), BF16)
            vpad[0:ATT_WINDOW, :] = jnp.zeros((ATT_WINDOW, MIX), BF16)
        kpad[ATT_WINDOW:ATT_WINDOW + seq, :] = k_ref[0].astype(BF16)
        vpad[ATT_WINDOW:ATT_WINDOW + seq, :] = v_ref[0].astype(BF16)

    win = ATT_WINDOW + qb
    start = pl.multiple_of(c * qb, qb)
    kwin = kpad[pl.ds(start, win), :]
    vwin = vpad[pl.ds(start, win), :]
    q = q_ref[0].astype(BF16)
    if not has_prefix:
        col = lax.broadcasted_iota(jnp.int32, (qb, win), 1)
        valid = col >= (ATT_WINDOW - c * qb)
    outs = []
    for h in range(ATT_HEADS):
        hs = slice(h * ATT_HEAD_DIM, (h + 1) * ATT_HEAD_DIM)
        s = _dot_nt(q[:, hs], kwin[:, hs]) + bias_ref[h]
        if not has_prefix:
            s = jnp.where(valid, s, -jnp.inf)
        m = jnp.max(s, axis=-1, keepdims=True)
        e = jnp.exp(s - m)
        p = e * (1.0 / jnp.sum(e, axis=-1, keepdims=True))
        outs.append(_dot(p.astype(BF16), vwin[:, hs]))
    o_ref[0] = jnp.concatenate(outs, axis=-1)


def _attn_bias(table, qb):
    d = ATT_WINDOW + jnp.arange(CHUNK)[:, None] - jnp.arange(ATT_BAND)[None, :]
    band = table.astype(F32)[:, jnp.clip(d, -REL_MAX, REL_MAX) + REL_MAX]
    rows = [jnp.pad(band, ((0, 0), (0, 0), (i * CHUNK, qb - CHUNK - i * CHUNK)), constant_values=-jnp.inf)
            for i in range(qb // CHUNK)]
    return jnp.concatenate(rows, axis=1)


def _attn(q, k, v, table, prefix=None):
    b, l, _ = q.shape
    qb = _tile(l, 256)
    has_prefix = prefix is not None
    full = pl.BlockSpec((1, l, MIX), lambda i, j: (i, 0, 0))
    in_specs = [pl.BlockSpec((1, qb, MIX), lambda i, j: (i, j, 0)), full, full]
    args = [q, k, v]
    if has_prefix:
        pre = pl.BlockSpec((1, ATT_WINDOW, MIX), lambda i, j: (i, 0, 0))
        in_specs += [pre, pre]
        args += list(prefix)
    in_specs.append(pl.BlockSpec((ATT_HEADS, qb, ATT_WINDOW + qb), lambda i, j: (0, 0, 0)))
    args.append(_attn_bias(table, qb))
    return pl.pallas_call(
        functools.partial(_attn_kernel, seq=l, qb=qb, has_prefix=has_prefix),
        grid=(b, l // qb),
        in_specs=in_specs,
        out_specs=pl.BlockSpec((1, qb, MIX), lambda i, j: (i, j, 0)),
        out_shape=jax.ShapeDtypeStruct((b, l, MIX), F32),
        scratch_shapes=[pltpu.VMEM((ATT_WINDOW + l, MIX), BF16),
                        pltpu.VMEM((ATT_WINDOW + l, MIX), BF16)],
        compiler_params=_cparams(2),
        name="attn",
    )(*args)


def _gelu_tanh(x):
    return 0.5 * x * (1.0 + jnp.tanh(math.sqrt(2.0 / math.pi) * (x + 0.044715 * (x * x * x))))


def _s5_kernel(u_ref, h0_ref, bb_ref, cc_ref, a_ref, d_ref, gw_ref, gb_ref, y_ref, hout_ref, xs, hst, *, tc):
    j = pl.program_id(1)

    @pl.when(j == 0)
    def _():
        hst[...] = h0_ref[0]

    u = u_ref[0]
    xs[...] = _dot(u.astype(BF16), bb_ref[...])
    ar = a_ref[:, :S5_MODES]
    ai = a_ref[:, S5_MODES:]

    def step(t, carry):
        hr, hi = carry
        r0 = pl.multiple_of(t * S5_STREAMS, S5_STREAMS)
        xr = xs[pl.ds(r0, S5_STREAMS), 0:S5_MODES]
        xi = xs[pl.ds(r0, S5_STREAMS), S5_MODES:2 * S5_MODES]
        nr = ar * hr - ai * hi + xr
        ni = ar * hi + ai * hr + xi
        xs[pl.ds(r0, S5_STREAMS), 0:S5_MODES] = nr
        xs[pl.ds(r0, S5_STREAMS), S5_MODES:2 * S5_MODES] = ni
        return nr, ni

    hr, hi = lax.fori_loop(0, tc, step, (hst[:, :S5_MODES], hst[:, S5_MODES:]))
    hst[:, :S5_MODES] = hr
    hst[:, S5_MODES:] = hi

    y = _dot(xs[...].astype(BF16), cc_ref[...]) + d_ref[...] * u
    gl = _dot(_gelu_tanh(y).astype(BF16), gw_ref[...]) + gb_ref[...]
    y_ref[0] = gl[:, :MIX] * _sigmoid(gl[:, MIX:])

    @pl.when(j == pl.num_programs(1) - 1)
    def _():
        hout_ref[0] = hst[...]


def _s5(u5, h0, bb, cc, a8, dvec, gw, gb, tc):
    nb, rows, _ = u5.shape
    blk = tc * S5_STREAMS
    const = lambda shape: pl.BlockSpec(shape, lambda i, j: (0,) * len(shape))
    return pl.pallas_call(
        functools.partial(_s5_kernel, tc=tc),
        grid=(nb, rows // blk),
        in_specs=[pl.BlockSpec((1, blk, MIX), lambda i, j: (i, j, 0)),
                  pl.BlockSpec((1, S5_STREAMS, 2 * S5_MODES), lambda i, j: (i, 0, 0)),
                  const((MIX, 2 * S5_MODES)), const((2 * S5_MODES, MIX)), const((S5_STREAMS, 2 * S5_MODES)),
                  const((1, MIX)), const((MIX, 2 * MIX)), const((1, 2 * MIX))],
        out_specs=[pl.BlockSpec((1, blk, MIX), lambda i, j: (i, j, 0)),
                   pl.BlockSpec((1, S5_STREAMS, 2 * S5_MODES), lambda i, j: (i, 0, 0))],
        out_shape=[jax.ShapeDtypeStruct((nb, rows, MIX), F32),
                   jax.ShapeDtypeStruct((nb, S5_STREAMS, 2 * S5_MODES), F32)],
        scratch_shapes=[pltpu.VMEM((blk, 2 * S5_MODES), F32),
                        pltpu.VMEM((S5_STREAMS, 2 * S5_MODES), F32)],
        compiler_params=_cparams(2),
        name="s5",
    )(u5, h0, bb, cc, a8, dvec, gw, gb)


def _out_router_kernel(x_ref, ya_ref, yb_ref, yc_ref, yd_ref, wout_ref, g_ref, wr_ref, tril_ref,
                       h_ref, hn_ref, route_ref, cnt_ref, base, *, tm):
    first = jnp.logical_and(pl.program_id(0) == 0, pl.program_id(1) == 0)

    @pl.when(first)
    def _():
        base[...] = jnp.zeros_like(base)

    mix = jnp.concatenate([ya_ref[0], yb_ref[0], yc_ref[0], yd_ref[0]], axis=-1).astype(BF16)
    h = x_ref[0] + _dot(mix, wout_ref[...])
    h_ref[0] = h
    hb = _rms(h, g_ref[...]).astype(BF16)
    hn_ref[0] = _pack_bf16_pairs(hb)
    lg = _dot(hb, wr_ref[...])

    lane = lax.broadcasted_iota(jnp.int32, (tm, LANES), 1).astype(F32)
    big = float(LANES)
    is_grp = lane < MOE_GROUPS
    lgg = jnp.where(is_grp, lg, -jnp.inf)
    mg = jnp.max(lgg, axis=-1, keepdims=True)
    grp = jnp.min(jnp.where(lgg == mg, lane, big), axis=-1, keepdims=True)
    p_grp = 1.0 / jnp.sum(jnp.where(is_grp, jnp.exp(lgg - mg), 0.0), axis=-1, keepdims=True)
    lo = ROUTE_LANE0 + MOE_EPG * grp
    in_grp = jnp.logical_and(lane >= lo, lane < lo + MOE_EPG)
    le = jnp.where(in_grp, lg, -jnp.inf)
    t1 = jnp.max(le, axis=-1, keepdims=True)
    i1 = jnp.min(jnp.where(le == t1, lane, big), axis=-1, keepdims=True)
    le2 = jnp.where(lane == i1, -jnp.inf, le)
    t2 = jnp.max(le2, axis=-1, keepdims=True)
    i2 = jnp.min(jnp.where(le2 == t2, lane, big), axis=-1, keepdims=True)
    e21 = jnp.exp(t2 - t1)
    inv = 1.0 / (1.0 + e21)
    g1 = inv * p_grp
    g2 = (e21 * inv) * p_grp

    oh1 = lane == i1
    oh2 = lane == i2
    ohs = jnp.where(jnp.logical_or(oh1, oh2), 1.0, 0.0)
    cnt = _dot(tril_ref[...], ohs.astype(BF16)) + base[...]
    r1 = jnp.sum(jnp.where(oh1, cnt, 0.0), axis=-1, keepdims=True)
    r2 = jnp.sum(jnp.where(oh2, cnt, 0.0), axis=-1, keepdims=True)
    base[...] = base[...] + jnp.sum(ohs, axis=0, keepdims=True)

    route = jnp.where(lane == 0.0, i1 - ROUTE_LANE0, 0.0)
    route = jnp.where(lane == 1.0, i2 - ROUTE_LANE0, route)
    route = jnp.where(lane == 2.0, r1, route)
    route = jnp.where(lane == 3.0, r2, route)
    route = jnp.where(lane == 4.0, g1, route)
    route = jnp.where(lane == 5.0, g2, route)
    route_ref[0] = route
    cnt_ref[...] = jnp.broadcast_to(base[...], (8, LANES))


def _out_router(x, ya, yb, yc, yd5, wout, g, wr, tril, tm):
    b, l, _ = x.shape
    row = lambda width: pl.BlockSpec((1, tm, width), lambda i, j: (i, j, 0))
    const = lambda shape: pl.BlockSpec(shape, lambda i, j: (0,) * len(shape))
    return pl.pallas_call(
        functools.partial(_out_router_kernel, tm=tm),
        grid=(b, l // tm),
        in_specs=[row(D_MODEL), row(MIX), row(MIX), row(MIX),
                  pl.BlockSpec((1, tm, MIX), lambda i, j: (i // S5_STREAMS, j, i % S5_STREAMS)),
                  const((D_MODEL, D_MODEL)), const((1, D_MODEL)), const((D_MODEL, LANES)), const((tm, tm))],
        out_specs=[row(D_MODEL), row(HALF), row(LANES), const((8, LANES))],
        out_shape=[jax.ShapeDtypeStruct((b, l, D_MODEL), F32),
                   jax.ShapeDtypeStruct((b, l, HALF), jnp.uint32),
                   jax.ShapeDtypeStruct((b, l, LANES), F32),
                   jax.ShapeDtypeStruct((8, LANES), F32)],
        scratch_shapes=[pltpu.VMEM((1, LANES), F32)],
        compiler_params=_cparams(2),
        name="out_router",
    )(x, ya, yb, yc, yd5, wout, g, wr, tril)


HALF = D_MODEL // 2
HI_MASK = 0xFFFF0000


def _pack_bf16_pairs(xb):
    bits = lax.bitcast_convert_type(xb.astype(F32), jnp.uint32)
    return (bits[:, :HALF] >> 16) | (bits[:, HALF:] & jnp.uint32(HI_MASK))


def _unpack_bf16_pairs(w):
    lo = lax.bitcast_convert_type(w << 16, F32)
    hi = lax.bitcast_convert_type(w & jnp.uint32(HI_MASK), F32)
    return lo.astype(BF16), hi.astype(BF16)


def _row_copy(src, src_row, dst, dst_row, sem):
    return pltpu.make_async_copy(src.at[pl.ds(src_row, 1), :], dst.at[pl.ds(dst_row, 1), :], sem)


def _dispatch_kernel(dest_ref, hn_ref, xs_in_ref, xs_ref, sem, *, tm):
    del xs_in_ref

    def issue(blk, carry):
        for u in range(DMA_UNROLL):
            i = blk * DMA_UNROLL + u
            for kk in range(2):
                _row_copy(hn_ref, i, xs_ref, dest_ref[0, 0, 2 * i + kk], sem).start()
        return carry

    lax.fori_loop(0, tm // DMA_UNROLL, issue, 0)
    def drain(blk, carry):
        for _ in range(2 * DMA_UNROLL):
            _row_copy(hn_ref, 0, xs_ref, 0, sem).wait()
        return carry

    lax.fori_loop(0, tm // DMA_UNROLL, drain, 0)


def _dispatch(dest3, hn, xs0, tm):
    t = hn.shape[0]
    return pl.pallas_call(
        functools.partial(_dispatch_kernel, tm=tm),
        grid=(t // tm,),
        in_specs=[pl.BlockSpec((1, 1, 2 * tm), lambda i: (i, 0, 0), memory_space=pltpu.SMEM),
                  pl.BlockSpec((tm, HALF), lambda i: (i, 0)),
                  pl.BlockSpec(memory_space=pl.ANY)],
        out_specs=pl.BlockSpec(memory_space=pl.ANY),
        out_shape=jax.ShapeDtypeStruct(xs0.shape, xs0.dtype),
        scratch_shapes=[pltpu.SemaphoreType.DMA(())],
        input_output_aliases={2: 0},
        compiler_params=_cparams(1),
        name="dispatch",
    )(dest3, hn, xs0)


def _ffn_kernel(be_ref, xs_ref, wg_ref, wu_ref, wd_ref, ys_ref):
    del be_ref
    lo, hi = _unpack_bf16_pairs(xs_ref[...])
    a = _dot(lo, wg_ref[0, :HALF, :]) + _dot(hi, wg_ref[0, HALF:, :])
    v = _dot(lo, wu_ref[0, :HALF, :]) + _dot(hi, wu_ref[0, HALF:, :])
    ys_ref[...] = _dot((_silu(a) * v).astype(BF16), wd_ref[0])


def _ffn(blk_expert, xs, wg, wu, wd):
    n_rows = xs.shape[0]
    grid_spec = pltpu.PrefetchScalarGridSpec(
        num_scalar_prefetch=1,
        grid=(n_rows // MOE_ROWS,),
        in_specs=[pl.BlockSpec((MOE_ROWS, HALF), lambda i, be: (i, 0)),
                  pl.BlockSpec((1, D_MODEL, MOE_D_FF), lambda i, be: (be[i], 0, 0)),
                  pl.BlockSpec((1, D_MODEL, MOE_D_FF), lambda i, be: (be[i], 0, 0)),
                  pl.BlockSpec((1, MOE_D_FF, D_MODEL), lambda i, be: (be[i], 0, 0))],
        out_specs=pl.BlockSpec((MOE_ROWS, D_MODEL), lambda i, be: (i, 0)),
    )
    return pl.pallas_call(
        _ffn_kernel,
        grid_spec=grid_spec,
        out_shape=jax.ShapeDtypeStruct((n_rows, D_MODEL), F32),
        compiler_params=_cparams(1),
        name="expert_ffn",
    )(blk_expert, xs, wg, wu, wd)


def _combine_kernel(dest_ref, h_ref, route_ref, gfin_ref, ys_ref, out_ref, buf0, buf1, sem, *, tm, final):
    def issue(blk, carry):
        for u in range(DMA_UNROLL):
            i = blk * DMA_UNROLL + u
            _row_copy(ys_ref, dest_ref[0, 0, 2 * i], buf0, i, sem).start()
            _row_copy(ys_ref, dest_ref[0, 0, 2 * i + 1], buf1, i, sem).start()
        return carry

    lax.fori_loop(0, tm // DMA_UNROLL, issue, 0)
    def drain(blk, carry):
        for _ in range(2 * DMA_UNROLL):
            _row_copy(ys_ref, 0, buf0, 0, sem).wait()
        return carry

    lax.fori_loop(0, tm // DMA_UNROLL, drain, 0)
    route = route_ref[...]
    h = h_ref[...] + (route[:, 4:5] * buf0[...] + route[:, 5:6] * buf1[...])
    out_ref[...] = _rms(h, gfin_ref[...]) if final else h


def _combine(dest3, h, route, gfin, ys, tm, final):
    t = h.shape[0]
    return pl.pallas_call(
        functools.partial(_combine_kernel, tm=tm, final=final),
        grid=(t // tm,),
        in_specs=[pl.BlockSpec((1, 1, 2 * tm), lambda i: (i, 0, 0), memory_space=pltpu.SMEM),
                  pl.BlockSpec((tm, D_MODEL), lambda i: (i, 0)),
                  pl.BlockSpec((tm, LANES), lambda i: (i, 0)),
                  pl.BlockSpec((1, D_MODEL), lambda i: (0, 0)),
                  pl.BlockSpec(memory_space=pl.ANY)],
        out_specs=pl.BlockSpec((tm, D_MODEL), lambda i: (i, 0)),
        out_shape=jax.ShapeDtypeStruct((t, D_MODEL), F32),
        scratch_shapes=[pltpu.VMEM((tm, D_MODEL), F32), pltpu.VMEM((tm, D_MODEL), F32),
                        pltpu.SemaphoreType.DMA(())],
        compiler_params=_cparams(1),
        name="combine",
    )(dest3, h, route, gfin, ys)


def _prep_layer(p):
    w_in = p["w_in"]
    z, xbc, dt, ccm, q, k, v, s5 = jnp.split(
        w_in, [256, 768, 772, 1284, 1540, 1796, 2052], axis=1)
    dt = jnp.pad(dt, ((0, 0), (0, LANES - SSD_HEADS)))
    w_in_p = jnp.concatenate([z, xbc, ccm, q, k, v, s5, dt], axis=1).astype(BF16)

    pad_l = lambda a: jnp.pad(a, (0, LANES - a.shape[0]))[None, :]

    lr, li = p["s5_lam_re"], p["s5_lam_im"]
    dts = jnp.exp(p["s5_log_dt"])[:, None]
    mag = jnp.exp(lr * dts)
    ar, ai = mag * jnp.cos(li * dts), mag * jnp.sin(li * dts)
    den = lr * lr + li * li
    cr = ((ar - 1.0) * lr + ai * li) / den
    ci = (ai * lr - (ar - 1.0) * li) / den
    br, bi = p["s5_b_re"], p["s5_b_im"]
    bbr = cr[..., None] * br - ci[..., None] * bi
    bbi = cr[..., None] * bi + ci[..., None] * br
    eye = jnp.eye(S5_GROUPS, dtype=F32)
    drive = lambda m: jnp.einsum("gnc,gk->gckn", m, eye).reshape(MIX, S5_MODES)
    bb = jnp.concatenate([drive(bbr), drive(bbi)], axis=1).astype(BF16)
    read = lambda m: jnp.einsum("gcn,gk->gnkc", m, eye).reshape(S5_MODES, MIX)
    cc = jnp.concatenate([read(p["s5_c_re"]), -read(p["s5_c_im"])], axis=0).astype(BF16)
    a8 = jnp.broadcast_to(jnp.concatenate([ar.reshape(-1), ai.reshape(-1)])[None, :],
                          (S5_STREAMS, 2 * S5_MODES))

    w_route = jnp.concatenate(
        [p["moe_w_group"], jnp.transpose(p["moe_w_expert"], (1, 0, 2)).reshape(D_MODEL, MOE_EXPERTS)], axis=1)
    w_route = jnp.pad(w_route, ((0, 0), (0, LANES - w_route.shape[1]))).astype(BF16)
    return dict(
        g_mix=p["g_mix"][None, :], w_in=w_in_p,
        ssd_cw=p["ssd_conv_w"], ssd_cb=p["ssd_conv_b"][None, :], ssd_dtb=pad_l(p["ssd_dt_bias"]),
        ssd_alog=pad_l(p["ssd_a_log"]), ssd_dfull=jnp.repeat(p["ssd_d"], SSD_HEAD_DIM)[None, :],
        ssd_nw=p["ssd_norm_w"][None, :],
        ccm_w=jnp.pad(p["ccm_dw_w"], ((0, CCM_PAD - CCM_WIDTH), (0, 0))), ccm_b=p["ccm_dw_b"][None, :],
        ccm_lng=p["ccm_ln_g"][None, :], ccm_lnb=p["ccm_ln_b"][None, :],
        ccm_pw=p["ccm_pw_w"].astype(BF16), ccm_pwb=p["ccm_pw_b"][None, :],
        att_table=p["att_rel_bias"],
        s5_bb=bb, s5_cc=cc, s5_a8=a8, s5_d=p["s5_d"].reshape(1, MIX),
        s5_gw=p["s5_glu_w"].astype(BF16), s5_gb=p["s5_glu_b"][None, :],
        w_out=p["w_out"].astype(BF16), g_ffn=p["g_ffn"][None, :], w_route=w_route,
        moe_wg=p["moe_w_gate"].astype(BF16), moe_wu=p["moe_w_up"].astype(BF16),
        moe_wd=p["moe_w_down"].astype(BF16),
    )


def _tile(n, pref):
    return pref if n % pref == 0 else n


def _layer(x, w, st, g_final, final):
    b, l, _ = x.shape
    t = b * l
    nb = b // S5_STREAMS
    tm = _tile(l, 512)
    z, xbc, dt, ccm, q, k, v, u5 = _in_proj(x, w["g_mix"], w["w_in"], tm)

    tl = _tile(l, 256)
    hist8 = jnp.pad(st["ssd_conv"], ((0, 0), (8 - (SSD_CONV - 1), 0), (0, 0)))
    y_a, ssd_new = _ssd(z, xbc, dt, hist8, st["ssd"], w["ssd_cw"], w["ssd_cb"], w["ssd_dtb"],
                        w["ssd_alog"], w["ssd_dfull"], w["ssd_nw"], tl)
    ssd_conv_new = jnp.concatenate([st["ssd_conv"], xbc], axis=1)[:, -(SSD_CONV - 1):]

    hist32 = jnp.pad(st["ccm_conv"], ((0, 0), (CCM_PAD - (CCM_WIDTH - 1), 0), (0, 0)))
    y_b, ccm_tail = _ccm(ccm, hist32, w["ccm_w"], w["ccm_b"], w["ccm_lng"], w["ccm_lnb"],
                         w["ccm_pw"], w["ccm_pwb"], tl)
    ccm_conv_new = ccm_tail[:, CCM_PAD - (CCM_WIDTH - 1):]

    if "att_k" in st:
        prefix = (st["att_k"].reshape(b, ATT_WINDOW, MIX), st["att_v"].reshape(b, ATT_WINDOW, MIX))
        y_c = _attn(q, k, v, w["att_table"], prefix)
    else:
        y_c = _attn(q, k, v, w["att_table"])
    keep = min(ATT_WINDOW, l)
    k_new = k[:, l - keep:].reshape(b, keep, ATT_HEADS, ATT_HEAD_DIM)
    v_new = v[:, l - keep:].reshape(b, keep, ATT_HEADS, ATT_HEAD_DIM)

    h0 = jnp.concatenate([st["s5_re"].reshape(nb, S5_STREAMS, S5_MODES),
                          st["s5_im"].reshape(nb, S5_STREAMS, S5_MODES)], axis=-1)
    y_d5, s5_new = _s5(u5.reshape(nb, l * S5_STREAMS, MIX), h0, w["s5_bb"], w["s5_cc"], w["s5_a8"],
                       w["s5_d"], w["s5_gw"], w["s5_gb"], _tile(l, 64))
    s5_re_new = s5_new[:, :, :S5_MODES].reshape(b, S5_GROUPS, S5_STATE)
    s5_im_new = s5_new[:, :, S5_MODES:].reshape(b, S5_GROUPS, S5_STATE)

    tril = jnp.tril(jnp.ones((tm, tm), BF16), -1)
    h, hn, route, cnt = _out_router(x, y_a, y_b, y_c, y_d5.reshape(nb, l, S5_STREAMS * MIX),
                                    w["w_out"], w["g_ffn"], w["w_route"], tril, tm)

    route = route.reshape(t, LANES)
    counts = cnt[0, ROUTE_LANE0:ROUTE_LANE0 + MOE_EXPERTS].astype(jnp.int32)
    padded = (counts + MOE_ROWS - 1) // MOE_ROWS * MOE_ROWS
    pad_end = jnp.cumsum(padded)
    pad_start = pad_end - padded
    n_blocks = -(-(2 * t) // MOE_ROWS) + MOE_EXPERTS
    experts = jnp.arange(MOE_EXPERTS, dtype=jnp.int32)
    is_e = route[:, 0:2].astype(jnp.int32)[..., None] == experts
    dest = jnp.sum(jnp.where(is_e, pad_start, 0), axis=-1) + route[:, 2:4].astype(jnp.int32)
    blk_row0 = jnp.arange(n_blocks, dtype=jnp.int32) * MOE_ROWS
    blk_expert = jnp.minimum(jnp.sum((pad_end[None, :] <= blk_row0[:, None]).astype(jnp.int32), axis=1),
                             MOE_EXPERTS - 1)

    td = _tile(t, 256)
    dest3 = dest.reshape(t // td, 1, 2 * td)
    xs = _dispatch(dest3, hn.reshape(t, HALF), jnp.zeros((n_blocks * MOE_ROWS, HALF), jnp.uint32), td)
    ys = _ffn(blk_expert, xs, w["moe_wg"], w["moe_wu"], w["moe_wd"])
    out = _combine(dest3, h.reshape(t, D_MODEL), route, g_final, ys, td, final).reshape(b, l, D_MODEL)

    new = dict(ssd=ssd_new, ssd_conv=ssd_conv_new, ccm_conv=ccm_conv_new, att_k=k_new, att_v=v_new,
               s5_re=s5_re_new, s5_im=s5_im_new)
    return out, new


NAMES = ("ssd", "ssd_conv", "ccm_conv", "att_k", "att_v", "s5_re", "s5_im")


def kernel(x_prompt, x_sample, state_ssd, state_ssd_conv, state_ccm_conv, cache_att_k, cache_att_v, state_s5_re, state_s5_im, g_mix, w_in, ssd_conv_w, ssd_conv_b, ssd_dt_bias, ssd_a_log, ssd_d, ssd_norm_w, ccm_dw_w, ccm_dw_b, ccm_ln_g, ccm_ln_b, ccm_pw_w, ccm_pw_b, att_rel_bias, s5_lam_re, s5_lam_im, s5_log_dt, s5_b_re, s5_b_im, s5_c_re, s5_c_im, s5_d, s5_glu_w, s5_glu_b, w_out, g_ffn, moe_w_group, moe_w_expert, moe_w_gate, moe_w_up, moe_w_down, g_final):
    params = dict(g_mix=g_mix, w_in=w_in, ssd_conv_w=ssd_conv_w, ssd_conv_b=ssd_conv_b,
                  ssd_dt_bias=ssd_dt_bias, ssd_a_log=ssd_a_log, ssd_d=ssd_d, ssd_norm_w=ssd_norm_w,
                  ccm_dw_w=ccm_dw_w, ccm_dw_b=ccm_dw_b, ccm_ln_g=ccm_ln_g, ccm_ln_b=ccm_ln_b,
                  ccm_pw_w=ccm_pw_w, ccm_pw_b=ccm_pw_b, att_rel_bias=att_rel_bias, s5_lam_re=s5_lam_re,
                  s5_lam_im=s5_lam_im, s5_log_dt=s5_log_dt, s5_b_re=s5_b_re, s5_b_im=s5_b_im,
                  s5_c_re=s5_c_re, s5_c_im=s5_c_im, s5_d=s5_d, s5_glu_w=s5_glu_w, s5_glu_b=s5_glu_b,
                  w_out=w_out, g_ffn=g_ffn, moe_w_group=moe_w_group, moe_w_expert=moe_w_expert,
                  moe_w_gate=moe_w_gate, moe_w_up=moe_w_up, moe_w_down=moe_w_down)
    depth = w_in.shape[0]
    bp = x_prompt.shape[0]
    g_fin = g_final[None, :]
    new_p = {n: [] for n in NAMES}
    new_s = {n: [] for n in NAMES}
    yp, ys = x_prompt, x_sample
    for i in range(depth):
        w = _prep_layer({n: a[i] for n, a in params.items()})
        final = i == depth - 1
        st_p = dict(ssd=jnp.zeros((bp, SSD_HEADS, SSD_HEAD_DIM, SSD_STATE), F32),
                    ssd_conv=jnp.zeros((bp, SSD_CONV - 1, SSD_CONV_CH), F32),
                    ccm_conv=jnp.zeros((bp, CCM_WIDTH - 1, MIX), F32),
                    s5_re=jnp.zeros((bp, S5_GROUPS, S5_STATE), F32),
                    s5_im=jnp.zeros((bp, S5_GROUPS, S5_STATE), F32))
        yp, sp = _layer(yp, w, st_p, g_fin, final)
        st_s = dict(ssd=state_ssd[i], ssd_conv=state_ssd_conv[i], ccm_conv=state_ccm_conv[i],
                    att_k=cache_att_k[i], att_v=cache_att_v[i], s5_re=state_s5_re[i], s5_im=state_s5_im[i])
        ys, ss = _layer(ys, w, st_s, g_fin, final)
        for n in NAMES:
            new_p[n].append(sp[n])
            new_s[n].append(ss[n])
    sp = {n: jnp.stack(new_p[n]) for n in NAMES}
    ss = {n: jnp.stack(new_s[n]) for n in NAMES}
    return (yp, ys, sp["ssd"], ss["ssd"], sp["ssd_conv"], ss["ssd_conv"], sp["ccm_conv"], ss["ccm_conv"],
            sp["att_k"], ss["att_k"], sp["att_v"], ss["att_v"], sp["s5_re"], ss["s5_re"],
            sp["s5_im"], ss["s5_im"])
```
